```python
import math
import jax, jax.numpy as jnp
from jax import lax
import numpy as np

D_MODEL = 1024
BATCH = 4
SEQ = 4096
DEPTH = 4
DEC_BATCH = 128
DEC_SEQ = 1
PAST_LEN = 8192
PAGE_SIZE = 128

N_META = 16
EPS = 1e-6
D_FF = 2816
N_HEADS = 8
N_KV = 2
HEAD_DIM = 64
Q_PER_KV = N_HEADS // N_KV
WINDOW = 128
BLOCK = 128
ROPE_DIM = HEAD_DIM // 4
ROPE_THETA = 500000.0
SCALE = HEAD_DIM ** -0.5
C_CONV = 512
CONV_W = 31
D_SSM = D_MODEL
SSM_GROUP = 16
N_GROUPS = D_SSM // SSM_GROUP
SSM_STATE = 64
N_EVEN = (DEPTH + 1) // 2
N_ODD = DEPTH // 2
ATT_W = N_HEADS * HEAD_DIM
KV_W = N_KV * HEAD_DIM
IN_AB = ATT_W + 2 * KV_W + 2 * C_CONV
OUT_AB = ATT_W + C_CONV

kernel_name = 'hybrid_conformer_swa_s5_macaron_step'


def rmsnorm(x, g):
    xf = x.astype(jnp.float32)
    y = xf * lax.rsqrt(jnp.mean(xf * xf, axis=-1, keepdims=True) + EPS)
    return (y * g.astype(jnp.float32)).astype(x.dtype)


def layernorm(x, g, b):
    xf = x.astype(jnp.float32)
    mu = jnp.mean(xf, axis=-1, keepdims=True)
    xc = xf - mu
    y = xc * lax.rsqrt(jnp.mean(xc * xc, axis=-1, keepdims=True) + EPS)
    return (y * g.astype(jnp.float32) + b.astype(jnp.float32)).astype(x.dtype)


def swiglu(h, w_gu, w_down):
    gate, up = jnp.split(h @ w_gu, 2, axis=-1)
    return (jax.nn.silu(gate) * up) @ w_down


def rope_partial(x, pos):
    half = ROPE_DIM // 2
    inv = ROPE_THETA ** (-jnp.arange(half, dtype=jnp.float32) * 2.0 / ROPE_DIM)
    ang = pos.astype(jnp.float32)[:, None] * inv[None, :]
    cos = jnp.cos(ang)[:, None, :]
    sin = jnp.sin(ang)[:, None, :]
    xr = x[..., :ROPE_DIM].astype(jnp.float32)
    x1, x2 = xr[..., :half], xr[..., half:]
    rot = jnp.concatenate([x1 * cos - x2 * sin, x2 * cos + x1 * sin], axis=-1).astype(x.dtype)
    return jnp.concatenate([rot, x[..., ROPE_DIM:]], axis=-1)


def sink_probs(scores, mask, sink):
    s = jnp.where(mask, scores, -jnp.inf)
    sk = jnp.broadcast_to(sink.astype(jnp.float32)[:, :, None, None], s.shape[:-1] + (1,))
    p = jax.nn.softmax(jnp.concatenate([s, sk], axis=-1), axis=-1)
    return p[..., :-1]


def ab_project(h, pos, w_in):
    bsz, t, _ = h.shape
    proj = h @ w_in
    o1 = ATT_W
    o2 = o1 + KV_W
    o3 = o2 + KV_W
    o4 = o3 + C_CONV
    q = rope_partial(proj[..., :o1].reshape(bsz, t, N_HEADS, HEAD_DIM), pos)
    k = rope_partial(proj[..., o1:o2].reshape(bsz, t, N_KV, HEAD_DIM), pos)
    v = proj[..., o2:o3].reshape(bsz, t, N_KV, HEAD_DIM)
    z = proj[..., o3:o4] * jax.nn.sigmoid(proj[..., o4:])
    return q, k, v, z


def conv_branch(z, ctx, conv_w, conv_b, ln_g, ln_b):
    zf = jnp.concatenate([ctx.astype(z.dtype), z], axis=1)
    y = lax.conv_general_dilated(zf, conv_w.astype(z.dtype)[:, None, :], (1,), 'VALID',
                                 dimension_numbers=('NWC', 'WIO', 'NWC'),
                                 feature_group_count=C_CONV) + conv_b
    y = jax.nn.silu(layernorm(y, ln_g, ln_b))
    return y, zf[:, -(CONV_W - 1):]


def swa_prompt(q, k, v, sink):
    bsz, L = q.shape[0], q.shape[1]
    pad = (-L) % BLOCK
    lp = L + pad
    nb = lp // BLOCK
    padt = lambda t: jnp.pad(t, ((0, 0), (pad, 0), (0, 0), (0, 0)))
    qb = padt(q).reshape(bsz, nb, BLOCK, N_KV, Q_PER_KV, HEAD_DIM)
    kb = padt(k).reshape(bsz, nb, BLOCK, N_KV, HEAD_DIM)
    vb = padt(v).reshape(bsz, nb, BLOCK, N_KV, HEAD_DIM)
    prev = lambda t: jnp.pad(t[:, :-1], ((0, 0), (1, 0), (0, 0), (0, 0), (0, 0)))
    kk = jnp.concatenate([prev(kb), kb], axis=2)
    vv = jnp.concatenate([prev(vb), vb], axis=2)
    posb = (jnp.arange(lp, dtype=jnp.int32) - pad).reshape(nb, BLOCK)
    kpos = jnp.concatenate([posb - BLOCK, posb], axis=-1)
    diff = posb[:, :, None] - kpos[:, None, :]
    mask = (diff >= 0) & (diff <= WINDOW) & (kpos[:, None, :] >= 0)
    scores = jnp.einsum('bnqkgd,bnskd->bnkgqs', qb.astype(jnp.float32) * SCALE, kk.astype(jnp.float32))
    p = sink_probs(scores, mask[None, :, None, None], sink.reshape(N_KV, Q_PER_KV))
    o = jnp.einsum('bnkgqs,bnskd->bnqkgd', p.astype(vv.dtype), vv)
    o = o.reshape(bsz, lp, ATT_W)[:, pad:]
    keep = min(WINDOW, PAST_LEN)
    return o, k[:, -keep:], v[:, -keep:]


def swa_sample(q, k, v, buf_k, buf_v, sink):
    bsz, t = q.shape[0], q.shape[1]
    keep = buf_k.shape[1]
    kk = jnp.concatenate([buf_k.astype(k.dtype), k], axis=1)
    vv = jnp.concatenate([buf_v.astype(v.dtype), v], axis=1)
    kpos = PAST_LEN - keep + jnp.arange(keep + t, dtype=jnp.int32)
    qpos = PAST_LEN + jnp.arange(t, dtype=jnp.int32)
    diff = qpos[:, None] - kpos[None, :]
    mask = (diff >= 0) & (diff <= WINDOW)
    qg = q.reshape(bsz, t, N_KV, Q_PER_KV, HEAD_DIM)
    scores = jnp.einsum('btkgd,bskd->bkgts', qg.astype(jnp.float32) * SCALE, kk.astype(jnp.float32))
    p = sink_probs(scores, mask[None, None, None], sink.reshape(N_KV, Q_PER_KV))
    o = jnp.einsum('bkgts,bskd->btkgd', p.astype(vv.dtype), vv).reshape(bsz, t, ATT_W)
    return o, kk[:, -keep:], vv[:, -keep:]


def ab_mixer_prompt(h, pos, w_in, w_out, sink, conv_w, conv_b, ln_g, ln_b):
    q, k, v, z = ab_project(h, pos, w_in)
    ctx = jnp.zeros((h.shape[0], CONV_W - 1, C_CONV), z.dtype)
    c_out, c_state = conv_branch(z, ctx, conv_w, conv_b, ln_g, ln_b)
    a_out, k_state, v_state = swa_prompt(q, k, v, sink)
    out = jnp.concatenate([a_out, c_out], axis=-1) @ w_out
    return out, c_state, k_state, v_state


def ab_mixer_sample(h, pos, conv_ctx, buf_k, buf_v, w_in, w_out, sink, conv_w, conv_b, ln_g, ln_b):
    q, k, v, z = ab_project(h, pos, w_in)
    c_out, c_state = conv_branch(z, conv_ctx, conv_w, conv_b, ln_g, ln_b)
    a_out, k_state, v_state = swa_sample(q, k, v, buf_k, buf_v, sink)
    out = jnp.concatenate([a_out, c_out], axis=-1) @ w_out
    return out, c_state, k_state, v_state


def s5_scan(u, s_re, s_im, a_re, a_im, log_dt, b_re, b_im, c_re, c_im, d_skip):
    bsz, t, _ = u.shape
    f32 = jnp.float32
    a_re = a_re.astype(f32)
    a_im = a_im.astype(f32)
    dt = jnp.exp(log_dt.astype(f32))[:, None]
    mag = jnp.exp(a_re * dt)
    ang = a_im * dt
    lam_re = mag * jnp.cos(ang)
    lam_im = mag * jnp.sin(ang)
    den = a_re * a_re + a_im * a_im
    nr = lam_re - 1.0
    f_re = (nr * a_re + lam_im * a_im) / den
    f_im = (lam_im * a_re - nr * a_im) / den
    b_re = b_re.astype(f32)
    b_im = b_im.astype(f32)
    bb_re = f_re[..., None] * b_re - f_im[..., None] * b_im
    bb_im = f_re[..., None] * b_im + f_im[..., None] * b_re
    ug = u.astype(f32).reshape(bsz, t, N_GROUPS, SSM_GROUP)
    x_re = jnp.einsum('btgc,gpc->btgp', ug, bb_re)
    x_im = jnp.einsum('btgc,gpc->btgp', ug, bb_im)
    s_re = s_re.astype(f32)
    s_im = s_im.astype(f32)
    x_re = x_re.at[:, 0].add(lam_re * s_re - lam_im * s_im)
    x_im = x_im.at[:, 0].add(lam_re * s_im + lam_im * s_re)
    la_re = jnp.broadcast_to(lam_re, (1, t) + lam_re.shape)
    la_im = jnp.broadcast_to(lam_im, (1, t) + lam_im.shape)

    def combine(e, l):
        ear, eai, ebr, ebi = e
        lar, lai, lbr, lbi = l
        return (lar * ear - lai * eai, lar * eai + lai * ear,
                lar * ebr - lai * ebi + lbr, lar * ebi + lai * ebr + lbi)

    _, _, h_re, h_im = lax.associative_scan(combine, (la_re, la_im, x_re, x_im), axis=1)
    y = (jnp.einsum('btgp,gcp->btgc', h_re, c_re.astype(f32))
         - jnp.einsum('btgp,gcp->btgc', h_im, c_im.astype(f32)))
    y = y.reshape(bsz, t, D_SSM) + d_skip.astype(f32) * u.astype(f32)
    return y.astype(u.dtype), h_re[:, -1], h_im[:, -1]


def ssm_mixer(h, s_re, s_im, w_in, a_re, a_im, log_dt, b_re, b_im, c_re, c_im, d_skip, w_glu):
    u = h @ w_in
    y, n_re, n_im = s5_scan(u, s_re, s_im, a_re, a_im, log_dt, b_re, b_im, c_re, c_im, d_skip)
    val, gate = jnp.split(jax.nn.gelu(y) @ w_glu, 2, axis=-1)
    return val * jax.nn.sigmoid(gate), n_re, n_im


def setup_inputs(seed: int = 0) -> dict:
    key = jax.random.key(seed)
    ks = iter(jax.random.split(key, 40))
    nrm = lambda shape, s: jax.random.normal(next(ks), shape, jnp.float32) * s
    keep = min(WINDOW, PAST_LEN)
    a_im0 = jnp.broadcast_to(math.pi * jnp.arange(SSM_STATE, dtype=jnp.float32), (N_ODD, N_GROUPS, SSM_STATE))
    return {
        'x_prompt': nrm((BATCH, SEQ, D_MODEL), 1.0),
        'x_sample': nrm((DEC_BATCH, DEC_SEQ, D_MODEL), 1.0),
        'state_conv': nrm((N_EVEN, DEC_BATCH, CONV_W - 1, C_CONV), 0.5),
        'cache_win_k': nrm((N_EVEN, DEC_BATCH, keep, N_KV, HEAD_DIM), 1.0),
        'cache_win_v': nrm((N_EVEN, DEC_BATCH, keep, N_KV, HEAD_DIM), 1.0),
        'state_ssm_re': nrm((N_ODD, DEC_BATCH, N_GROUPS, SSM_STATE), 0.1),
        'state_ssm_im': nrm((N_ODD, DEC_BATCH, N_GROUPS, SSM_STATE), 0.1),
        'meta_tokens': nrm((N_META, D_MODEL), 1.0),
        'norm_g': 1.0 + nrm((DEPTH, 3, D_MODEL), 0.02),
        'final_norm_g': 1.0 + nrm((D_MODEL,), 0.02),
        'ffn1_w_gu': nrm((DEPTH, D_MODEL, 2 * D_FF), D_MODEL ** -0.5),
        'ffn1_w_down': nrm((DEPTH, D_FF, D_MODEL), D_FF ** -0.5),
        'ffn2_w_gu': nrm((DEPTH, D_MODEL, 2 * D_FF), D_MODEL ** -0.5),
        'ffn2_w_down': nrm((DEPTH, D_FF, D_MODEL), D_FF ** -0.5),
        'ab_w_in': nrm((N_EVEN, D_MODEL, IN_AB), D_MODEL ** -0.5),
        'ab_w_out': nrm((N_EVEN, OUT_AB, D_MODEL), OUT_AB ** -0.5),
        'attn_sink': nrm((N_EVEN, N_HEADS), 0.5),
        'conv_w': nrm((N_EVEN, CONV_W, C_CONV), CONV_W ** -0.5),
        'conv_b': nrm((N_EVEN, C_CONV), 0.02),
        'conv_ln_g': 1.0 + nrm((N_EVEN, C_CONV), 0.02),
        'conv_ln_b': nrm((N_EVEN, C_CONV), 0.02),
        'ssm_w_in': nrm((N_ODD, D_MODEL, D_SSM), D_MODEL ** -0.5),
        'ssm_a_re': -0.5 * jnp.exp(nrm((N_ODD, N_GROUPS, SSM_STATE), 0.02)),
        'ssm_a_im': a_im0 + nrm((N_ODD, N_GROUPS, SSM_STATE), 0.01),
        'ssm_log_dt': jax.random.uniform(next(ks), (N_ODD, N_GROUPS), jnp.float32, math.log(1e-3), math.log(1e-1)),
        'ssm_b_re': nrm((N_ODD, N_GROUPS, SSM_STATE, SSM_GROUP), (2 * SSM_GROUP) ** -0.5),
        'ssm_b_im': nrm((N_ODD, N_GROUPS, SSM_STATE, SSM_GROUP), (2 * SSM_GROUP) ** -0.5),
        'ssm_c_re': nrm((N_ODD, N_GROUPS, SSM_GROUP, SSM_STATE), SSM_STATE ** -0.5),
        'ssm_c_im': nrm((N_ODD, N_GROUPS, SSM_GROUP, SSM_STATE), SSM_STATE ** -0.5),
        'ssm_d': nrm((N_ODD, D_SSM), 1.0),
        'ssm_w_glu': nrm((N_ODD, D_SSM, 2 * D_MODEL), D_SSM ** -0.5),
    }


def reference(x_prompt, x_sample, state_conv, cache_win_k, cache_win_v, state_ssm_re, state_ssm_im,
              meta_tokens, norm_g, final_norm_g, ffn1_w_gu, ffn1_w_down, ffn2_w_gu, ffn2_w_down,
              ab_w_in, ab_w_out, attn_sink, conv_w, conv_b, conv_ln_g, conv_ln_b,
              ssm_w_in, ssm_a_re, ssm_a_im, ssm_log_dt, ssm_b_re, ssm_b_im, ssm_c_re, ssm_c_im,
              ssm_d, ssm_w_glu):
    bp = x_prompt.shape[0]
    meta = jnp.broadcast_to(meta_tokens.astype(x_prompt.dtype)[None], (bp, N_META, D_MODEL))
    xp = jnp.concatenate([meta, x_prompt], axis=1)
    xs = x_sample
    pos_p = jnp.arange(xp.shape[1], dtype=jnp.int32)
    pos_s = PAST_LEN + jnp.arange(xs.shape[1], dtype=jnp.int32)
    p_conv, p_k, p_v, p_re, p_im = [], [], [], [], []
    s_conv, s_k, s_v, s_re, s_im = [], [], [], [], []
    for l in range(DEPTH):
        xp = xp + 0.5 * swiglu(rmsnorm(xp, norm_g[l, 0]), ffn1_w_gu[l], ffn1_w_down[l])
        xs = xs + 0.5 * swiglu(rmsnorm(xs, norm_g[l, 0]), ffn1_w_gu[l], ffn1_w_down[l])
        hp = rmsnorm(xp, norm_g[l, 1])
        hs = rmsnorm(xs, norm_g[l, 1])
        i = l // 2
        if l % 2 == 0:
            w = (ab_w_in[i], ab_w_out[i], attn_sink[i], conv_w[i], conv_b[i], conv_ln_g[i], conv_ln_b[i])
            mp, cp, kp, vp = ab_mixer_prompt(hp, pos_p, *w)
            ms, cs, kS, vS = ab_mixer_sample(hs, pos_s, state_conv[i], cache_win_k[i], cache_win_v[i], *w)
            p_conv.append(cp)
            p_k.append(kp)
            p_v.append(vp)
            s_conv.append(cs)
            s_k.append(kS)
            s_v.append(vS)
        else:
            w = (ssm_w_in[i], ssm_a_re[i], ssm_a_im[i], ssm_log_dt[i], ssm_b_re[i], ssm_b_im[i],
                 ssm_c_re[i], ssm_c_im[i], ssm_d[i], ssm_w_glu[i])
            zeros = jnp.zeros((bp, N_GROUPS, SSM_STATE), jnp.float32)
            mp, rp, ip = ssm_mixer(hp, zeros, zeros, *w)
            ms, rS, iS = ssm_mixer(hs, state_ssm_re[i], state_ssm_im[i], *w)
            p_re.append(rp)
            p_im.append(ip)
            s_re.append(rS)
            s_im.append(iS)
        xp = xp + mp
        xs = xs + ms
        xp = xp + 0.5 * swiglu(rmsnorm(xp, norm_g[l, 2]), ffn2_w_gu[l], ffn2_w_down[l])
        xs = xs + 0.5 * swiglu(rmsnorm(xs, norm_g[l, 2]), ffn2_w_gu[l], ffn2_w_down[l])
    y_prompt = rmsnorm(xp, final_norm_g)[:, N_META:]
    y_sample = rmsnorm(xs, final_norm_g)
    return (y_prompt, y_sample,
            jnp.stack(p_conv), jnp.stack(p_k), jnp.stack(p_v), jnp.stack(p_re), jnp.stack(p_im),
            jnp.stack(s_conv), jnp.stack(s_k), jnp.stack(s_v), jnp.stack(s_re), jnp.stack(s_im))
```

```python
import functools
import math

import jax
import jax.numpy as jnp
from jax import lax
from jax.experimental import pallas as pl
from jax.experimental.pallas import tpu as pltpu

f32 = jnp.float32
bf16 = jnp.bfloat16

D_MODEL = 1024
BATCH = 4
SEQ = 4096
DEPTH = 4
DEC_BATCH = 128
PAST_LEN = 8192
N_META = 16
EPS = 1e-6
D_FF = 2816
N_HEADS = 8
N_KV = 2
HEAD_DIM = 64
WINDOW = 128
BLOCK = 128
ROPE_DIM = HEAD_DIM // 4
ROPE_THETA = 500000.0
SCALE = HEAD_DIM ** -0.5
C_CONV = 512
CONV_W = 31
SSM_GROUP = 16
N_GROUPS = D_MODEL // SSM_GROUP
SSM_STATE = 64
ATT_W = N_HEADS * HEAD_DIM
KV_W = N_KV * HEAD_DIM
IN_AB = ATT_W + 2 * KV_W + 2 * C_CONV
N_CPLX = N_GROUPS * SSM_STATE

LANES = 128
SUBLANES = 8
VMEM_LIMIT = 56 * 1024 * 1024

L_PROMPT = N_META + SEQ
PAD_ROWS = (-L_PROMPT) % BLOCK
ROWS_B = PAD_ROWS + L_PROMPT
R_PROMPT = BATCH * ROWS_B
R_ALL = R_PROMPT + DEC_BATCH

TM_FFN = 448
FF_CHUNK = 256
N_FF_CHUNKS = D_FF // FF_CHUNK
TT_AB = 384
SCAN_SEGS = SUBLANES
SEG_LEN = 66
TT_SSM = SCAN_SEGS * SEG_LEN
N_QUART = 4
Q_CH = D_MODEL // N_QUART
Q_ST = N_CPLX // N_QUART
SB_SAMPLE = 32
NEG_BIG = -1e30


def _cparams(n_axes):
    return pltpu.CompilerParams(dimension_semantics=("arbitrary",) * n_axes, vmem_limit_bytes=VMEM_LIMIT)


def _resident(shape):
    zeros = (0,) * len(shape)
    return pl.BlockSpec(shape, lambda *_: zeros, pipeline_mode=pl.Buffered(1))


def _rms(x, g):
    ms = jnp.mean(x * x, axis=-1, keepdims=True)
    return x * lax.rsqrt(ms + EPS) * g


def _dot(a, b):
    return jnp.dot(a, b, preferred_element_type=f32)


def _dot_nt(a, b):
    return lax.dot_general(a, b, (((1,), (1,)), ((), ())), preferred_element_type=f32)


def _ffn_kernel(x_ref, g_ref, wg_ref, wu_ref, wd_ref, o_ref, h_ref, acc_ref):
    x = x_ref[...]
    h_ref[...] = _rms(x, g_ref[...]).astype(bf16)
    acc_ref[...] = jnp.zeros_like(acc_ref)

    def chunk(c, carry):
        h = h_ref[...]
        gate = _dot(h, wg_ref[c])
        up = _dot(h, wu_ref[c])
        act = (gate * jax.nn.sigmoid(gate) * up).astype(bf16)
        acc_ref[...] += _dot(act, wd_ref[c])
        return carry

    lax.fori_loop(0, N_FF_CHUNKS, chunk, 0)
    o_ref[...] = x + 0.5 * acc_ref[...]


def _ffn(x, g, wg, wu, wd):
    return pl.pallas_call(
        _ffn_kernel,
        out_shape=jax.ShapeDtypeStruct((R_ALL, D_MODEL), f32),
        grid=(R_ALL // TM_FFN,),
        in_specs=[
            pl.BlockSpec((TM_FFN, D_MODEL), lambda i: (i, 0)),
            _resident((1, D_MODEL)),
            _resident((N_FF_CHUNKS, D_MODEL, FF_CHUNK)),
            _resident((N_FF_CHUNKS, D_MODEL, FF_CHUNK)),
            _resident((N_FF_CHUNKS, FF_CHUNK, D_MODEL)),
        ],
        out_specs=pl.BlockSpec((TM_FFN, D_MODEL), lambda i: (i, 0)),
        scratch_shapes=[pltpu.VMEM((TM_FFN, D_MODEL), bf16), pltpu.VMEM((TM_FFN, D_MODEL), f32)],
        input_output_aliases={0: 0},
        compiler_params=_cparams(1),
        name="ffn",
    )(x, g, wg, wu, wd)


def _prep_ffn(w_gu, w_down):
    wg = w_gu[:, :D_FF].reshape(D_MODEL, N_FF_CHUNKS, FF_CHUNK).transpose(1, 0, 2).astype(bf16)
    wu = w_gu[:, D_FF:].reshape(D_MODEL, N_FF_CHUNKS, FF_CHUNK).transpose(1, 0, 2).astype(bf16)
    wd = w_down.reshape(N_FF_CHUNKS, FF_CHUNK, D_MODEL).astype(bf16)
    return wg, wu, wd


def _norm_kernel(x_ref, g_ref, o_ref):
    o_ref[...] = _rms(x_ref[...], g_ref[...]).reshape(o_ref.shape)


def _final_norm(x, g):
    blocks_b = ROWS_B // BLOCK
    y_prompt = pl.pallas_call(
        _norm_kernel,
        out_shape=jax.ShapeDtypeStruct((BATCH, SEQ, D_MODEL), f32),
        grid=(BATCH, SEQ // BLOCK),
        in_specs=[pl.BlockSpec((BLOCK, D_MODEL), lambda b, j: (b * blocks_b + 1 + j, 0)), _resident((1, D_MODEL))],
        out_specs=pl.BlockSpec((1, BLOCK, D_MODEL), lambda b, j: (b, j, 0)),
        compiler_params=_cparams(2),
        name="final_norm_prompt",
    )(x, g)
    y_sample = pl.pallas_call(
        _norm_kernel,
        out_shape=jax.ShapeDtypeStruct((DEC_BATCH, D_MODEL), f32),
        grid=(1,),
        in_specs=[pl.BlockSpec((DEC_BATCH, D_MODEL), lambda i: (R_PROMPT // DEC_BATCH, 0)), _resident((1, D_MODEL))],
        out_specs=pl.BlockSpec((DEC_BATCH, D_MODEL), lambda i: (0, 0)),
        compiler_params=_cparams(1),
        name="final_norm_sample",
    )(x, g)
    return y_prompt, y_sample.reshape(DEC_BATCH, 1, D_MODEL)


def _rope(t, cos_p, sin_p, first):
    half = ROPE_DIM // 2
    partner = jnp.where(first, pltpu.roll(t, LANES - half, 1), pltpu.roll(t, half, 1))
    return t * cos_p + partner * sin_p


def _layernorm_silu(y, g, b):
    mu = jnp.mean(y, axis=-1, keepdims=True)
    yc = y - mu
    yn = yc * lax.rsqrt(jnp.mean(yc * yc, axis=-1, keepdims=True) + EPS) * g + b
    return yn * jax.nn.sigmoid(yn)


def _softmax_sink(s, sink):
    m = jnp.maximum(jnp.max(s, axis=-1, keepdims=True), sink)
    e = jnp.exp(s - m)
    den = jnp.sum(e, axis=-1, keepdims=True) + jnp.exp(sink - m)
    return e / den


def _ab_prompt_kernel(sink_ref, x_ref, g_ref, win_ref, wout_ref, cw_ref, cb_ref, lng_ref, lnb_ref, cos_ref, sin_ref,
                      bias_ref, o_ref, cst_ref, kst_ref, vst_ref, kk_ref, vv_ref, z_ref, a_ref):
    tt = TT_AB
    i = pl.program_id(1)

    @pl.when(i == 0)
    def _():
        kk_ref[:, 0:BLOCK, :] = jnp.zeros((N_KV, BLOCK, LANES), bf16)
        vv_ref[:, 0:BLOCK, :] = jnp.zeros((N_KV, BLOCK, LANES), bf16)
        z_ref[0:32, :] = jnp.zeros((32, C_CONV), f32)

    x = x_ref[...]
    h = _rms(x, g_ref[...]).astype(bf16)
    proj = _dot(h, win_ref[...])

    cos_p = cos_ref[...]
    sin_p = sin_ref[...]
    lane = lax.broadcasted_iota(jnp.int32, (tt, LANES), 1)
    first = (lane % HEAD_DIM) < (ROPE_DIM // 2)
    lo = lane < HEAD_DIM

    q = [_rope(proj[:, c * LANES:(c + 1) * LANES], cos_p, sin_p, first) for c in range(ATT_W // LANES)]
    k = _rope(proj[:, ATT_W:ATT_W + KV_W], cos_p, sin_p, first)
    v = proj[:, ATT_W + KV_W:ATT_W + 2 * KV_W]
    k_sw = pltpu.roll(k, HEAD_DIM, 1)
    v_sw = pltpu.roll(v, HEAD_DIM, 1)
    kk_ref[0, BLOCK:, :] = jnp.where(lo, k, k_sw).astype(bf16)
    kk_ref[1, BLOCK:, :] = jnp.where(lo, k_sw, k).astype(bf16)
    vv_ref[0, BLOCK:, :] = jnp.where(lo, v, v_sw).astype(bf16)
    vv_ref[1, BLOCK:, :] = jnp.where(lo, v_sw, v).astype(bf16)

    o3 = ATT_W + 2 * KV_W
    z = proj[:, o3:o3 + C_CONV] * jax.nn.sigmoid(proj[:, o3 + C_CONV:o3 + 2 * C_CONV])
    z_ref[32:, :] = z

    bias = bias_ref[...]
    col = lax.broadcasted_iota(jnp.int32, (BLOCK, 2 * BLOCK), 1)
    lo_b = lax.broadcasted_iota(jnp.int32, (BLOCK, LANES), 1) < HEAD_DIM
    for jb in range(tt // BLOCK):
        kmin = PAD_ROWS + BLOCK - (i * tt + jb * BLOCK)
        key_ok = col >= kmin
        for grp in range(N_KV):
            k2 = kk_ref[grp, jb * BLOCK:(jb + 2) * BLOCK, :]
            v2 = vv_ref[grp, jb * BLOCK:(jb + 2) * BLOCK, :]
            for cpart in range(2):
                cblk = grp * 2 + cpart
                qc = q[cblk][jb * BLOCK:(jb + 1) * BLOCK]
                outs = []
                for half in range(2):
                    qm = jnp.where(lo_b if half == 0 else jnp.logical_not(lo_b), qc, 0.0).astype(bf16)
                    s = _dot_nt(qm, k2)
                    s = jnp.where(key_ok, s + bias, NEG_BIG)
                    p = _softmax_sink(s, sink_ref[cblk * 2 + half])
                    outs.append(_dot(p.astype(bf16), v2))
                a_ref[jb * BLOCK:(jb + 1) * BLOCK, cblk * LANES:(cblk + 1) * LANES] = (
                    jnp.where(lo_b, outs[0], outs[1]).astype(bf16))

    acc = jnp.broadcast_to(cb_ref[...], (tt, C_CONV))
    for w in range(CONV_W):
        acc = acc + z_ref[pl.ds(32 - (CONV_W - 1) + w, tt), :] * cw_ref[pl.ds(w, 1), :]
    c = _layernorm_silu(acc, lng_ref[...], lnb_ref[...])

    out = _dot(a_ref[...], wout_ref[0:ATT_W, :]) + _dot(c.astype(bf16), wout_ref[ATT_W:, :])
    row = i * tt + lax.broadcasted_iota(jnp.int32, (tt, 1), 0)
    o_ref[...] = jnp.where(row >= PAD_ROWS, x + out, 0.0)

    z_ref[0:32, :] = z_ref[tt:tt + 32, :]
    kk_ref[:, 0:BLOCK, :] = kk_ref[:, tt:tt + BLOCK, :]
    vv_ref[:, 0:BLOCK, :] = vv_ref[:, tt:tt + BLOCK, :]

    @pl.when(i == pl.num_programs(1) - 1)
    def _():
        kst_ref[0] = k[tt - BLOCK:]
        vst_ref[0] = v[tt - BLOCK:]
        cst_ref[0] = z_ref[pl.ds(tt + 32 - (CONV_W - 1), CONV_W - 1), :]


def _ab_prompt(x, sink, g, w_in, w_out, cw, cb, lng, lnb, cos_t, sin_t, bias):
    steps = ROWS_B // TT_AB
    return pl.pallas_call(
        _ab_prompt_kernel,
        out_shape=(
            jax.ShapeDtypeStruct((R_ALL, D_MODEL), f32),
            jax.ShapeDtypeStruct((BATCH, CONV_W - 1, C_CONV), f32),
            jax.ShapeDtypeStruct((BATCH, BLOCK, KV_W), f32),
            jax.ShapeDtypeStruct((BATCH, BLOCK, KV_W), f32),
        ),
        grid=(BATCH, steps),
        in_specs=[
            pl.BlockSpec(memory_space=pltpu.SMEM),
            pl.BlockSpec((TT_AB, D_MODEL), lambda b, i: (b * steps + i, 0)),
            _resident((1, D_MODEL)),
            _resident((D_MODEL, IN_AB)),
            _resident((ATT_W + C_CONV, D_MODEL)),
            _resident((CONV_W, C_CONV)),
            _resident((1, C_CONV)),
            _resident((1, C_CONV)),
            _resident((1, C_CONV)),
            pl.BlockSpec((TT_AB, LANES), lambda b, i: (i, 0)),
            pl.BlockSpec((TT_AB, LANES), lambda b, i: (i, 0)),
            _resident((BLOCK, 2 * BLOCK)),
        ],
        out_specs=(
            pl.BlockSpec((TT_AB, D_MODEL), lambda b, i: (b * steps + i, 0)),
            pl.BlockSpec((1, CONV_W - 1, C_CONV), lambda b, i: (b, 0, 0)),
            pl.BlockSpec((1, BLOCK, KV_W), lambda b, i: (b, 0, 0)),
            pl.BlockSpec((1, BLOCK, KV_W), lambda b, i: (b, 0, 0)),
        ),
        scratch_shapes=[
            pltpu.VMEM((N_KV, BLOCK + TT_AB, LANES), bf16),
            pltpu.VMEM((N_KV, BLOCK + TT_AB, LANES), bf16),
            pltpu.VMEM((32 + TT_AB, C_CONV), f32),
            pltpu.VMEM((TT_AB, ATT_W), bf16),
        ],
        input_output_aliases={1: 0},
        compiler_params=_cparams(2),
        name="ab_prompt",
    )(sink, x, g, w_in, w_out, cw, cb, lng, lnb, cos_t, sin_t, bias)


def _ab_sample_kernel(sink_ref, x_ref, g_ref, win_ref, wout_ref, cw_ref, cb_ref, lng_ref, lnb_ref, cos_ref, sin_ref,
                      ctx_ref, kc_ref, vc_ref, o_ref, cso_ref, ko_ref, vo_ref, q_scr, k_scr, v_scr, z_scr, y_scr, a_scr):
    sb = SB_SAMPLE
    x = x_ref[...]
    h = _rms(x, g_ref[...]).astype(bf16)
    proj = _dot(h, win_ref[...])

    cos_p = cos_ref[...]
    sin_p = sin_ref[...]
    lane = lax.broadcasted_iota(jnp.int32, (sb, LANES), 1)
    first = (lane % HEAD_DIM) < (ROPE_DIM // 2)
    qw = N_HEADS * LANES
    for hd in range(N_HEADS):
        q_scr[:, hd * LANES:(hd + 1) * LANES] = _rope(proj[:, hd * LANES:(hd + 1) * LANES], cos_p, sin_p, first)
    k_scr[...] = _rope(proj[:, qw:qw + KV_W], cos_p, sin_p, first)
    v_scr[...] = proj[:, qw + KV_W:qw + 2 * KV_W]
    o3 = qw + 2 * KV_W
    z = proj[:, o3:o3 + C_CONV] * jax.nn.sigmoid(proj[:, o3 + C_CONV:o3 + 2 * C_CONV])
    z_scr[...] = z

    sub = lax.broadcasted_iota(jnp.int32, (N_HEADS, LANES), 0)
    sink_col = jnp.zeros((N_HEADS, 1), f32)
    sub1 = lax.broadcasted_iota(jnp.int32, (N_HEADS, 1), 0)
    for hd in range(N_HEADS):
        sink_col = jnp.where(sub1 == hd, sink_ref[hd], sink_col)
    cw_ctx = cw_ref[0:CONV_W - 1, :]
    cw_last = cw_ref[CONV_W - 1:CONV_W, :]

    def per_seq(b, carry):
        qrow = q_scr[pl.ds(b, 1), :]
        lhs = jnp.zeros((N_HEADS, LANES), f32)
        for hd in range(N_HEADS):
            lhs = jnp.where(sub == hd, qrow[:, hd * LANES:(hd + 1) * LANES], lhs)
        k_new = k_scr[pl.ds(b, 1), :]
        v_new = v_scr[pl.ds(b, 1), :]
        kc = kc_ref[b]
        vc = vc_ref[b]
        s = _dot_nt(lhs.astype(bf16), kc.astype(bf16))
        s_new = jnp.sum(lhs * k_new, axis=-1, keepdims=True)
        m = jnp.maximum(jnp.maximum(jnp.max(s, axis=-1, keepdims=True), s_new), sink_col)
        e = jnp.exp(s - m)
        e_new = jnp.exp(s_new - m)
        den = jnp.sum(e, axis=-1, keepdims=True) + e_new + jnp.exp(sink_col - m)
        o = _dot((e / den).astype(bf16), vc.astype(bf16)) + (e_new / den) * v_new
        a_scr[pl.ds(pl.multiple_of(b * N_HEADS, SUBLANES), N_HEADS), :] = o
        ko_ref[b, 0:WINDOW - 1, :] = kc_ref[b, 1:WINDOW, :]
        ko_ref[b, WINDOW - 1:WINDOW, :] = k_new
        vo_ref[b, 0:WINDOW - 1, :] = vc_ref[b, 1:WINDOW, :]
        vo_ref[b, WINDOW - 1:WINDOW, :] = v_new
        z_row = z_scr[pl.ds(b, 1), :]
        ctx = ctx_ref[b]
        y_row = jnp.sum(ctx * cw_ctx, axis=0, keepdims=True) + z_row * cw_last
        for ks in range(C_CONV // LANES):
            y_scr[pl.ds(pl.multiple_of(ks * sb * SUBLANES + b * SUBLANES, SUBLANES), SUBLANES), :] = (
                jnp.broadcast_to(y_row[:, ks * LANES:(ks + 1) * LANES], (SUBLANES, LANES)))
        cso_ref[b, 0:CONV_W - 2, :] = ctx_ref[b, 1:CONV_W - 1, :]
        cso_ref[b, CONV_W - 2:CONV_W - 1, :] = z_row
        return carry

    lax.fori_loop(0, sb, per_seq, 0)

    y = jnp.concatenate([y_scr[pl.ds(ks * sb * SUBLANES, sb, stride=SUBLANES), :] for ks in range(C_CONV // LANES)],
                        axis=1)
    a = jnp.concatenate([a_scr[pl.ds(hd, sb, stride=N_HEADS), :] for hd in range(N_HEADS)], axis=1)
    c = _layernorm_silu(y + cb_ref[...], lng_ref[...], lnb_ref[...])
    out = _dot(a.astype(bf16), wout_ref[0:qw, :]) + _dot(c.astype(bf16), wout_ref[qw:, :])
    o_ref[...] = x + out


def _ab_sample(x, sink, g, w_in, w_out, cw, cb, lng, lnb, cos_s, sin_s, ctx, kc, vc):
    steps = DEC_BATCH // SB_SAMPLE
    first_blk = R_PROMPT // SB_SAMPLE
    qw = N_HEADS * LANES
    seq3 = lambda i: (i, 0, 0)
    return pl.pallas_call(
        _ab_sample_kernel,
        out_shape=(
            jax.ShapeDtypeStruct((R_ALL, D_MODEL), f32),
            jax.ShapeDtypeStruct((DEC_BATCH, CONV_W - 1, C_CONV), f32),
            jax.ShapeDtypeStruct((DEC_BATCH, WINDOW, KV_W), f32),
            jax.ShapeDtypeStruct((DEC_BATCH, WINDOW, KV_W), f32),
        ),
        grid=(steps,),
        in_specs=[
            pl.BlockSpec(memory_space=pltpu.SMEM),
            pl.BlockSpec((SB_SAMPLE, D_MODEL), lambda i: (first_blk + i, 0)),
            _resident((1, D_MODEL)),
            _resident((D_MODEL, qw + 2 * KV_W + 2 * C_CONV)),
            _resident((qw + C_CONV, D_MODEL)),
            _resident((CONV_W, C_CONV)),
            _resident((1, C_CONV)),
            _resident((1, C_CONV)),
            _resident((1, C_CONV)),
            _resident((1, LANES)),
            _resident((1, LANES)),
            pl.BlockSpec((SB_SAMPLE, CONV_W - 1, C_CONV), seq3),
            pl.BlockSpec((SB_SAMPLE, WINDOW, KV_W), seq3),
            pl.BlockSpec((SB_SAMPLE, WINDOW, KV_W), seq3),
        ],
        out_specs=(
            pl.BlockSpec((SB_SAMPLE, D_MODEL), lambda i: (first_blk + i, 0)),
            pl.BlockSpec((SB_SAMPLE, CONV_W - 1, C_CONV), seq3),
            pl.BlockSpec((SB_SAMPLE, WINDOW, KV_W), seq3),
            pl.BlockSpec((SB_SAMPLE, WINDOW, KV_W), seq3),
        ),
        scratch_shapes=[
            pltpu.VMEM((SB_SAMPLE, qw), f32),
            pltpu.VMEM((SB_SAMPLE, KV_W), f32),
            pltpu.VMEM((SB_SAMPLE, KV_W), f32),
            pltpu.VMEM((SB_SAMPLE, C_CONV), f32),
            pltpu.VMEM((C_CONV // LANES * SB_SAMPLE * SUBLANES, LANES), f32),
            pltpu.VMEM((SB_SAMPLE * N_HEADS, LANES), f32),
        ],
        input_output_aliases={1: 0},
        compiler_params=_cparams(1),
        name="ab_sample",
    )(sink, x, g, w_in, w_out, cw, cb, lng, lnb, cos_s, sin_s, ctx, kc, vc)


def _ssm_prep_kernel(are_ref, aim_ref, ldt_ref, bre_ref, bim_ref, lre_ref, lim_ref, pre_ref, pim_ref, bbre_ref, bbim_ref):
    a_re = are_ref[...]
    a_im = aim_ref[...]
    dt = jnp.exp(ldt_ref[...])
    mag = jnp.exp(a_re * dt)
    ang = a_im * dt
    lam_re = mag * jnp.cos(ang)
    lam_im = mag * jnp.sin(ang)
    den = a_re * a_re + a_im * a_im
    nr = lam_re - 1.0
    f_re = (nr * a_re + lam_im * a_im) / den
    f_im = (lam_im * a_re - nr * a_im) / den
    b_re = bre_ref[...]
    b_im = bim_ref[...]
    bbre_ref[...] = f_re * b_re - f_im * b_im
    bbim_ref[...] = f_re * b_im + f_im * b_re
    lre_ref[...] = lam_re
    lim_ref[...] = lam_im
    p_re, p_im = lam_re, lam_im
    r_re = r_im = None
    n = SEG_LEN
    while n:
        if n & 1:
            if r_re is None:
                r_re, r_im = p_re, p_im
            else:
                r_re, r_im = r_re * p_re - r_im * p_im, r_re * p_im + r_im * p_re
        n >>= 1
        if n:
            p_re, p_im = p_re * p_re - p_im * p_im, 2.0 * p_re * p_im
    pre_ref[...] = r_re
    pim_ref[...] = r_im


def _ssm_prep(a_re, a_im, log_dt, b_re, b_im, c_re, c_im):
    flat = lambda t: t.reshape(1, N_CPLX)
    ldt = jnp.repeat(log_dt, SSM_STATE).reshape(1, N_CPLX)
    b_t = lambda t: t.transpose(2, 0, 1).reshape(SSM_GROUP, N_CPLX)
    row = jax.ShapeDtypeStruct((1, N_CPLX), f32)
    mat = jax.ShapeDtypeStruct((SSM_GROUP, N_CPLX), f32)
    lre, lim, pre, pim, bbre, bbim = pl.pallas_call(
        _ssm_prep_kernel, out_shape=(row, row, row, row, mat, mat), name="ssm_prep",
    )(flat(a_re), flat(a_im), ldt, b_t(b_re), b_t(b_im))
    eye = jnp.eye(SSM_GROUP, dtype=f32)
    gq = N_GROUPS // N_QUART

    def in_mat(bb):
        t = bb.reshape(SSM_GROUP, N_QUART, gq, SSM_STATE).transpose(1, 2, 0, 3)
        return (t[:, :, :, None, :] * eye[None, :, None, :, None]).reshape(N_QUART, Q_CH, Q_ST).astype(bf16)

    def out_mat(cc):
        t = cc.reshape(N_QUART, gq, SSM_GROUP, SSM_STATE).transpose(0, 1, 3, 2)
        return (t[:, :, :, None, :] * eye[None, :, None, :, None]).reshape(N_QUART, Q_ST, Q_CH).astype(bf16)

    return lre, lim, pre, pim, in_mat(bbre), in_mat(bbim), out_mat(c_re), out_mat(c_im)


def _cmul_add(lr, li, hr, hi, xr, xi):
    return lr * hr - li * hi + xr, lr * hi + li * hr + xi


def _ssm_prompt_kernel(x_ref, g_ref, win_ref, bbre_ref, bbim_ref, ccre_ref, ccim_ref, lre_ref, lim_ref, pre_ref, pim_ref,
                       d_ref, wglu_ref, o_ref, sre_ref, sim_ref, slab_ref, perm_ref, xr_ref, xi_ref, y_ref, cr_ref, ci_ref):
    tt = TT_SSM
    i = pl.program_id(1)

    @pl.when(i == 0)
    def _():
        cr_ref[...] = jnp.zeros_like(cr_ref)
        ci_ref[...] = jnp.zeros_like(ci_ref)

    x = x_ref[...]
    h = _rms(x, g_ref[...]).astype(bf16)
    u = _dot(h, win_ref[...])

    n_slab = D_MODEL // LANES
    for ks in range(n_slab):
        slab_ref[ks * tt:(ks + 1) * tt, :] = u[:, ks * LANES:(ks + 1) * LANES]

    def gather(t, carry):
        for ks in range(n_slab):
            perm_ref[pl.ds(t * SCAN_SEGS, SCAN_SEGS), ks * LANES:(ks + 1) * LANES] = (
                slab_ref[pl.ds(ks * tt + t, SCAN_SEGS, stride=SEG_LEN), :])
        return carry

    lax.fori_loop(0, SEG_LEN, gather, 0)
    up = perm_ref[...]
    ub = up.astype(bf16)

    for qd in range(N_QUART):
        st = slice(qd * Q_ST, (qd + 1) * Q_ST)
        ubq = ub[:, qd * Q_CH:(qd + 1) * Q_CH]
        xr_ref[...] = _dot(ubq, bbre_ref[qd])
        xi_ref[...] = _dot(ubq, bbim_ref[qd])
        lr = jnp.broadcast_to(lre_ref[:, st], (SCAN_SEGS, Q_ST))
        li = jnp.broadcast_to(lim_ref[:, st], (SCAN_SEGS, Q_ST))

        def local_end(t, hc):
            rows = pl.ds(t * SCAN_SEGS, SCAN_SEGS)
            return _cmul_add(lr, li, hc[0], hc[1], xr_ref[rows, :], xi_ref[rows, :])

        zero = jnp.zeros((SCAN_SEGS, Q_ST), f32)
        er, ei = lax.fori_loop(0, SEG_LEN, local_end, (zero, zero))

        pr = pre_ref[:, st]
        pi = pim_ref[:, st]
        c_r = cr_ref[:, st]
        c_i = ci_ref[:, st]
        rows_r, rows_i = [c_r], [c_i]
        for j in range(1, SCAN_SEGS):
            c_r, c_i = _cmul_add(pr, pi, c_r, c_i, er[j - 1:j], ei[j - 1:j])
            rows_r.append(c_r)
            rows_i.append(c_i)
        c_r, c_i = _cmul_add(pr, pi, c_r, c_i, er[SCAN_SEGS - 1:], ei[SCAN_SEGS - 1:])
        cr_ref[:, st] = c_r
        ci_ref[:, st] = c_i
        start = (jnp.concatenate(rows_r, axis=0), jnp.concatenate(rows_i, axis=0))

        def scan(t, hc):
            rows = pl.ds(t * SCAN_SEGS, SCAN_SEGS)
            hr, hi = _cmul_add(lr, li, hc[0], hc[1], xr_ref[rows, :], xi_ref[rows, :])
            xr_ref[rows, :] = hr
            xi_ref[rows, :] = hi
            return hr, hi

        lax.fori_loop(0, SEG_LEN, scan, start)
        y_ref[:, qd * Q_CH:(qd + 1) * Q_CH] = (_dot(xr_ref[...].astype(bf16), ccre_ref[qd])
                                               - _dot(xi_ref[...].astype(bf16), ccim_ref[qd]))

    y = y_ref[...] + d_ref[...] * up
    vg = _dot(jax.nn.gelu(y).astype(bf16), wglu_ref[...])
    perm_ref[...] = vg[:, :D_MODEL] * jax.nn.sigmoid(vg[:, D_MODEL:])

    def scatter(t, carry):
        for ks in range(n_slab):
            slab_ref[pl.ds(ks * tt + t, SCAN_SEGS, stride=SEG_LEN), :] = (
                perm_ref[pl.ds(t * SCAN_SEGS, SCAN_SEGS), ks * LANES:(ks + 1) * LANES])
        return carry

    lax.fori_loop(0, SEG_LEN, scatter, 0)
    out = jnp.concatenate([slab_ref[ks * tt:(ks + 1) * tt, :] for ks in range(n_slab)], axis=1)
    row = i * tt + lax.broadcasted_iota(jnp.int32, (tt, 1), 0)
    o_ref[...] = jnp.where(row >= PAD_ROWS, x + out, 0.0)

    @pl.when(i == pl.num_programs(1) - 1)
    def _():
        sre_ref[0] = cr_ref[...]
        sim_ref[0] = ci_ref[...]


def _ssm_prompt(x, g, w_in, bbre, bbim, ccre, ccim, lre, lim, pre, pim, d, w_glu):
    steps = ROWS_B // TT_SSM
    state = jax.ShapeDtypeStruct((BATCH, 1, N_CPLX), f32)
    return pl.pallas_call(
        _ssm_prompt_kernel,
        out_shape=(jax.ShapeDtypeStruct((R_ALL, D_MODEL), f32), state, state),
        grid=(BATCH, steps),
        in_specs=[
            pl.BlockSpec((TT_SSM, D_MODEL), lambda b, i: (b * steps + i, 0)),
            _resident((1, D_MODEL)),
            _resident((D_MODEL, D_MODEL)),
            _resident((N_QUART, Q_CH, Q_ST)),
            _resident((N_QUART, Q_CH, Q_ST)),
            _resident((N_QUART, Q_ST, Q_CH)),
            _resident((N_QUART, Q_ST, Q_CH)),
            _resident((1, N_CPLX)),
            _resident((1, N_CPLX)),
            _resident((1, N_CPLX)),
            _resident((1, N_CPLX)),
            _resident((1, D_MODEL)),
            _resident((D_MODEL, 2 * D_MODEL)),
        ],
        out_specs=(
            pl.BlockSpec((TT_SSM, D_MODEL), lambda b, i: (b * steps + i, 0)),
            pl.BlockSpec((1, 1, N_CPLX), lambda b, i: (b, 0, 0)),
            pl.BlockSpec((1, 1, N_CPLX), lambda b, i: (b, 0, 0)),
        ),
        scratch_shapes=[
            pltpu.VMEM((D_MODEL // LANES * TT_SSM, LANES), f32),
            pltpu.VMEM((TT_SSM, D_MODEL), f32),
            pltpu.VMEM((TT_SSM, Q_ST), f32),
            pltpu.VMEM((TT_SSM, Q_ST), f32),
            pltpu.VMEM((TT_SSM, D_MODEL), f32),
            pltpu.VMEM((1, N_CPLX), f32),
            pltpu.VMEM((1, N_CPLX), f32),
        ],
        input_output_aliases={0: 0},
        compiler_params=_cparams(2),
        name="ssm_prompt",
    )(x, g, w_in, bbre, bbim, ccre, ccim, lre, lim, pre, pim, d, w_glu)


def _ssm_sample_kernel(x_ref, g_ref, win_ref, bbre_ref, bbim_ref, ccre_ref, ccim_ref, lre_ref, lim_ref, d_ref, wglu_ref,
                       sre_in, sim_in, o_ref, sre_ref, sim_ref, y_ref):
    x = x_ref[...]
    h = _rms(x, g_ref[...]).astype(bf16)
    u = _dot(h, win_ref[...])
    ub = u.astype(bf16)
    for qd in range(N_QUART):
        st = slice(qd * Q_ST, (qd + 1) * Q_ST)
        ubq = ub[:, qd * Q_CH:(qd + 1) * Q_CH]
        hr, hi = _cmul_add(lre_ref[:, st], lim_ref[:, st], sre_in[:, st], sim_in[:, st],
                           _dot(ubq, bbre_ref[qd]), _dot(ubq, bbim_ref[qd]))
        sre_ref[:, st] = hr
        sim_ref[:, st] = hi
        y_ref[:, qd * Q_CH:(qd + 1) * Q_CH] = _dot(hr.astype(bf16), ccre_ref[qd]) - _dot(hi.astype(bf16), ccim_ref[qd])
    y = y_ref[...] + d_ref[...] * u
    vg = _dot(jax.nn.gelu(y).astype(bf16), wglu_ref[...])
    o_ref[...] = x + vg[:, :D_MODEL] * jax.nn.sigmoid(vg[:, D_MODEL:])


def _ssm_sample(x, g, w_in, bbre, bbim, ccre, ccim, lre, lim, d, w_glu, s_re, s_im):
    state = jax.ShapeDtypeStruct((DEC_BATCH, N_CPLX), f32)
    blk = R_PROMPT // DEC_BATCH
    return pl.pallas_call(
        _ssm_sample_kernel,
        out_shape=(jax.ShapeDtypeStruct((R_ALL, D_MODEL), f32), state, state),
        grid=(1,),
        in_specs=[
            pl.BlockSpec((DEC_BATCH, D_MODEL), lambda i: (blk, 0)),
            _resident((1, D_MODEL)),
            _resident((D_MODEL, D_MODEL)),
            _resident((N_QUART, Q_CH, Q_ST)),
            _resident((N_QUART, Q_CH, Q_ST)),
            _resident((N_QUART, Q_ST, Q_CH)),
            _resident((N_QUART, Q_ST, Q_CH)),
            _resident((1, N_CPLX)),
            _resident((1, N_CPLX)),
            _resident((1, D_MODEL)),
            _resident((D_MODEL, 2 * D_MODEL)),
            _resident((DEC_BATCH, N_CPLX)),
            _resident((DEC_BATCH, N_CPLX)),
        ],
        out_specs=(
            pl.BlockSpec((DEC_BATCH, D_MODEL), lambda i: (blk, 0)),
            pl.BlockSpec((DEC_BATCH, N_CPLX), lambda i: (0, 0)),
            pl.BlockSpec((DEC_BATCH, N_CPLX), lambda i: (0, 0)),
        ),
        scratch_shapes=[pltpu.VMEM((DEC_BATCH, D_MODEL), f32)],
        input_output_aliases={0: 0},
        compiler_params=_cparams(1),
        name="ssm_sample",
    )(x, g, w_in, bbre, bbim, ccre, ccim, lre, lim, d, w_glu, s_re, s_im)


def _rope_tables():
    half = ROPE_DIM // 2
    inv = ROPE_THETA ** (-jnp.arange(half, dtype=f32) * 2.0 / ROPE_DIM)
    pos = jnp.concatenate([jnp.arange(ROWS_B, dtype=jnp.int32) - PAD_ROWS, jnp.full((1,), PAST_LEN, jnp.int32)])
    ang = pos.astype(f32)[:, None] * inv[None, :]
    cos, sin = jnp.cos(ang), jnp.sin(ang)
    ones = jnp.ones((pos.shape[0], HEAD_DIM - ROPE_DIM), f32)
    cos_h = jnp.concatenate([cos, cos, ones], axis=1)
    sin_h = jnp.concatenate([-sin, sin, 0.0 * ones], axis=1)
    cos_t = jnp.concatenate([cos_h, cos_h], axis=1)
    sin_t = jnp.concatenate([sin_h, sin_h], axis=1)
    return cos_t[:ROWS_B], sin_t[:ROWS_B], cos_t[ROWS_B:], sin_t[ROWS_B:]


def _window_bias():
    r = jnp.arange(BLOCK, dtype=jnp.int32)[:, None]
    s = jnp.arange(2 * BLOCK, dtype=jnp.int32)[None, :]
    return jnp.where((s >= r) & (s <= r + WINDOW), 0.0, NEG_BIG).astype(f32)


def _prep_ab(w_in, w_out):
    wq = w_in[:, :ATT_W] * SCALE
    rest = w_in[:, ATT_W:]
    w_in_p = jnp.concatenate([wq, rest], axis=1).astype(bf16)
    w_out_p = w_out.astype(bf16)
    zeros_in = jnp.zeros((D_MODEL, HEAD_DIM), f32)
    zeros_out = jnp.zeros((HEAD_DIM, D_MODEL), f32)
    q_cols, o_rows = [], []
    for hd in range(N_HEADS):
        wq_h = wq[:, hd * HEAD_DIM:(hd + 1) * HEAD_DIM]
        wo_h = w_out[hd * HEAD_DIM:(hd + 1) * HEAD_DIM]
        if hd // (N_HEADS // N_KV) == 0:
            q_cols += [wq_h, zeros_in]
            o_rows += [wo_h, zeros_out]
        else:
            q_cols += [zeros_in, wq_h]
            o_rows += [zeros_out, wo_h]
    w_in_s = jnp.concatenate(q_cols + [rest], axis=1).astype(bf16)
    w_out_s = jnp.concatenate(o_rows + [w_out[ATT_W:]], axis=0).astype(bf16)
    return w_in_p, w_out_p, w_in_s, w_out_s


def kernel(x_prompt, x_sample, state_conv, cache_win_k, cache_win_v, state_ssm_re, state_ssm_im, meta_tokens, norm_g, final_norm_g, ffn1_w_gu, ffn1_w_down, ffn2_w_gu, ffn2_w_down, ab_w_in, ab_w_out, attn_sink, conv_w, conv_b, conv_ln_g, conv_ln_b, ssm_w_in, ssm_a_re, ssm_a_im, ssm_log_dt, ssm_b_re, ssm_b_im, ssm_c_re, ssm_c_im, ssm_d, ssm_w_glu):
    meta = jnp.broadcast_to(meta_tokens[None], (BATCH, N_META, D_MODEL))
    pad = jnp.zeros((BATCH, PAD_ROWS, D_MODEL), f32)
    x = jnp.concatenate([jnp.concatenate([pad, meta, x_prompt], axis=1).reshape(R_PROMPT, D_MODEL),
                         x_sample.reshape(DEC_BATCH, D_MODEL)], axis=0)

    cos_t, sin_t, cos_s, sin_s = _rope_tables()
    bias = _window_bias()
    row = lambda t: t.reshape(1, -1)

    p_conv, p_k, p_v, p_re, p_im = [], [], [], [], []
    s_conv, s_k, s_v, s_re, s_im = [], [], [], [], []
    for l in range(DEPTH):
        x = _ffn(x, row(norm_g[l, 0]), *_prep_ffn(ffn1_w_gu[l], ffn1_w_down[l]))
        g_mix = row(norm_g[l, 1])
        i = l // 2
        if l % 2 == 0:
            w_in_p, w_out_p, w_in_s, w_out_s = _prep_ab(ab_w_in[i], ab_w_out[i])
            cargs = (conv_w[i], row(conv_b[i]), row(conv_ln_g[i]), row(conv_ln_b[i]))
            x, cst, kst, vst = _ab_prompt(x, attn_sink[i], g_mix, w_in_p, w_out_p, *cargs, cos_t, sin_t, bias)
            x, cso, kso, vso = _ab_sample(x, attn_sink[i], g_mix, w_in_s, w_out_s, *cargs, cos_s, sin_s,
                                          state_conv[i], cache_win_k[i].reshape(DEC_BATCH, WINDOW, KV_W),
                                          cache_win_v[i].reshape(DEC_BATCH, WINDOW, KV_W))
            p_conv.append(cst)
            p_k.append(kst.reshape(BATCH, WINDOW, N_KV, HEAD_DIM))
            p_v.append(vst.reshape(BATCH, WINDOW, N_KV, HEAD_DIM))
            s_conv.append(cso)
            s_k.append(kso.reshape(DEC_BATCH, WINDOW, N_KV, HEAD_DIM))
            s_v.append(vso.reshape(DEC_BATCH, WINDOW, N_KV, HEAD_DIM))
        else:
            lre, lim, pre, pim, bbre, bbim, ccre, ccim = _ssm_prep(
                ssm_a_re[i], ssm_a_im[i], ssm_log_dt[i], ssm_b_re[i], ssm_b_im[i], ssm_c_re[i], ssm_c_im[i])
            w_in = ssm_w_in[i].astype(bf16)
            w_glu = ssm_w_glu[i].astype(bf16)
            d = row(ssm_d[i])
            x, pre_s, pim_s = _ssm_prompt(x, g_mix, w_in, bbre, bbim, ccre, ccim, lre, lim, pre, pim, d, w_glu)
            x, sre_s, sim_s = _ssm_sample(x, g_mix, w_in, bbre, bbim, ccre, ccim, lre, lim, d, w_glu,
                                          state_ssm_re[i].reshape(DEC_BATCH, N_CPLX),
                                          state_ssm_im[i].reshape(DEC_BATCH, N_CPLX))
            p_re.append(pre_s.reshape(BATCH, N_GROUPS, SSM_STATE))
            p_im.append(pim_s.reshape(BATCH, N_GROUPS, SSM_STATE))
            s_re.append(sre_s.reshape(DEC_BATCH, N_GROUPS, SSM_STATE))
            s_im.append(sim_s.reshape(DEC_BATCH, N_GROUPS, SSM_STATE))
        x = _ffn(x, row(norm_g[l, 2]), *_prep_ffn(ffn2_w_gu[l], ffn2_w_down[l]))

    y_prompt, y_sample = _final_norm(x, row(final_norm_g))
    return (y_prompt, y_sample,
            jnp.stack(p_conv), jnp.stack(p_k), jnp.stack(p_v), jnp.stack(p_re), jnp.stack(p_im),
            jnp.stack(s_conv), jnp.stack(s_k), jnp.stack(s_v), jnp.stack(s_re), jnp.stack(s_im))
```

```python
import functools
import math

import jax
import jax.numpy as jnp
from jax import lax
from jax.experimental import pallas as pl
from jax.experimental.pallas import tpu as pltpu

f32 = jnp.float32
bf16 = jnp.bfloat16

D_MODEL = 1024
BATCH = 4
SEQ = 4096
DEPTH = 4
DEC_BATCH = 128
PAST_LEN = 8192
N_META = 16
EPS = 1e-6
D_FF = 2816
N_HEADS = 8
N_KV = 2
HEAD_DIM = 64
WINDOW = 128
BLOCK = 128
ROPE_DIM = HEAD_DIM // 4
ROPE_THETA = 500000.0
SCALE = HEAD_DIM ** -0.5
C_CONV = 512
CONV_W = 31
SSM_GROUP = 16
N_GROUPS = D_MODEL // SSM_GROUP
SSM_STATE = 64
ATT_W = N_HEADS * HEAD_DIM
KV_W = N_KV * HEAD_DIM
IN_AB = ATT_W + 2 * KV_W + 2 * C_CONV
N_CPLX = N_GROUPS * SSM_STATE
N_EVEN = (DEPTH + 1) // 2
N_ODD = DEPTH // 2

LANES = 128
SUBLANES = 8
VMEM_LIMIT = 56 * 1024 * 1024

L_PROMPT = N_META + SEQ
PAD_ROWS = (-L_PROMPT) % BLOCK
ROWS_B = PAD_ROWS + L_PROMPT
R_PROMPT = BATCH * ROWS_B
R_ALL = R_PROMPT + DEC_BATCH

TM_FFN = 448
TT_AB = 384
CONV_HIST = 32
CONV_ROWS = 64
SCAN_SEGS = SUBLANES
SEG_LEN = 66
TT_SSM = SCAN_SEGS * SEG_LEN
N_QUART = 4
Q_CH = D_MODEL // N_QUART
Q_ST = N_CPLX // N_QUART
NORM_BLOCKS = 8
SB_SAMPLE = 32
NEG_BIG = -1e30


def _cparams(n_axes):
    return pltpu.CompilerParams(dimension_semantics=("arbitrary",) * n_axes, vmem_limit_bytes=VMEM_LIMIT)


def _resident(shape):
    zeros = (0,) * len(shape)
    return pl.BlockSpec(shape, lambda *_: zeros, pipeline_mode=pl.Buffered(1))


def _layer(shape, layer):
    idx = (layer,) + (0,) * len(shape)
    return pl.BlockSpec((None,) + tuple(shape), lambda *_: idx, pipeline_mode=pl.Buffered(1))


def _rms(x, g):
    ms = jnp.mean(x * x, axis=-1, keepdims=True)
    return x * lax.rsqrt(ms + EPS) * g


def _dot(a, b):
    return jnp.dot(a, b, preferred_element_type=f32)


def _dot_nt(a, b):
    return lax.dot_general(a, b, (((1,), (1,)), ((), ())), preferred_element_type=f32)


def _ffn_kernel(norm_row, x_ref, g_ref, wgu_ref, wd_ref, o_ref):
    x = x_ref[...]
    h = _rms(x, g_ref[norm_row:norm_row + 1, :]).astype(bf16)
    gate = _dot(h, wgu_ref[:, :D_FF].astype(bf16))
    up = _dot(h, wgu_ref[:, D_FF:].astype(bf16))
    act = (gate * jax.nn.sigmoid(gate) * up).astype(bf16)
    o_ref[...] = x + 0.5 * _dot(act, wd_ref[...].astype(bf16))


def _ffn(x, norm_g, w_gu, w_down, layer, norm_row):
    return pl.pallas_call(
        functools.partial(_ffn_kernel, norm_row),
        out_shape=jax.ShapeDtypeStruct((R_ALL, D_MODEL), f32),
        grid=(R_ALL // TM_FFN,),
        in_specs=[
            pl.BlockSpec((TM_FFN, D_MODEL), lambda i: (i, 0)),
            _layer((3, D_MODEL), layer),
            _layer((D_MODEL, 2 * D_FF), layer),
            _layer((D_FF, D_MODEL), layer),
        ],
        out_specs=pl.BlockSpec((TM_FFN, D_MODEL), lambda i: (i, 0)),
        input_output_aliases={0: 0},
        compiler_params=_cparams(1),
        name="ffn",
    )(x, norm_g, w_gu, w_down)


def _norm_kernel(x_ref, g_ref, o_ref):
    o_ref[...] = _rms(x_ref[...], g_ref[...]).reshape(o_ref.shape)


def _final_norm(x, g):
    blocks_b = ROWS_B // BLOCK
    y_prompt = pl.pallas_call(
        _norm_kernel,
        out_shape=jax.ShapeDtypeStruct((BATCH, SEQ, D_MODEL), f32),
        grid=(BATCH, SEQ // BLOCK),
        in_specs=[pl.BlockSpec((BLOCK, D_MODEL), lambda b, j: (b * blocks_b + 1 + j, 0)), _resident((1, D_MODEL))],
        out_specs=pl.BlockSpec((1, BLOCK, D_MODEL), lambda b, j: (b, j, 0)),
        compiler_params=_cparams(2),
        name="final_norm_prompt",
    )(x, g)
    y_sample = pl.pallas_call(
        _norm_kernel,
        out_shape=jax.ShapeDtypeStruct((DEC_BATCH, D_MODEL), f32),
        grid=(1,),
        in_specs=[pl.BlockSpec((DEC_BATCH, D_MODEL), lambda i: (R_PROMPT // DEC_BATCH, 0)), _resident((1, D_MODEL))],
        out_specs=pl.BlockSpec((DEC_BATCH, D_MODEL), lambda i: (0, 0)),
        compiler_params=_cparams(1),
        name="final_norm_sample",
    )(x, g)
    return y_prompt, y_sample.reshape(DEC_BATCH, 1, D_MODEL)


def _rope(t, cos_p, sin_p, first):
    half = ROPE_DIM // 2
    partner = jnp.where(first, pltpu.roll(t, LANES - half, 1), pltpu.roll(t, half, 1))
    return t * cos_p + partner * sin_p


def _layernorm_silu(y, g, b):
    mu = jnp.mean(y, axis=-1, keepdims=True)
    yc = y - mu
    yn = yc * lax.rsqrt(jnp.mean(yc * yc, axis=-1, keepdims=True) + EPS) * g + b
    return yn * jax.nn.sigmoid(yn)


def _softmax_sink(s, sink):
    m = jnp.maximum(jnp.max(s, axis=-1, keepdims=True), sink)
    e = jnp.exp(s - m)
    den = jnp.sum(e, axis=-1, keepdims=True) + jnp.exp(sink - m)
    return e / den


def _conv_taps():
    taps = []
    for w in range(CONV_W):
        idx = CONV_HIST - (CONV_W - 1) + w
        taps.append((w, idx % SUBLANES, idx - idx % SUBLANES))
    return taps


def _ab_prompt_kernel(li, sink_ref, x_ref, g_ref, win_ref, wout_ref, cw_ref, cb_ref, lng_ref, lnb_ref, cos_ref, sin_ref,
                      bias_ref, o_ref, cst_ref, kst_ref, vst_ref, kk_ref, vv_ref, z_ref, zs_ref, y_ref, a_ref):
    tt = TT_AB
    i = pl.program_id(1)

    @pl.when(i == 0)
    def _():
        kk_ref[:, 0:BLOCK, :] = jnp.zeros((N_KV, BLOCK, LANES), bf16)
        vv_ref[:, 0:BLOCK, :] = jnp.zeros((N_KV, BLOCK, LANES), bf16)
        z_ref[0:CONV_HIST, :] = jnp.zeros((CONV_HIST, C_CONV), f32)

    x = x_ref[...]
    h = _rms(x, g_ref[1:2, :]).astype(bf16)
    proj = _dot(h, win_ref[...].astype(bf16))

    cos_p = cos_ref[...]
    sin_p = sin_ref[...]
    lane = lax.broadcasted_iota(jnp.int32, (tt, LANES), 1)
    first = (lane % HEAD_DIM) < (ROPE_DIM // 2)
    lo = lane < HEAD_DIM

    q = [_rope(proj[:, c * LANES:(c + 1) * LANES], cos_p, sin_p, first) * SCALE for c in range(ATT_W // LANES)]
    k = _rope(proj[:, ATT_W:ATT_W + KV_W], cos_p, sin_p, first)
    v = proj[:, ATT_W + KV_W:ATT_W + 2 * KV_W]
    k_sw = pltpu.roll(k, HEAD_DIM, 1)
    v_sw = pltpu.roll(v, HEAD_DIM, 1)
    kk_ref[0, BLOCK:, :] = jnp.where(lo, k, k_sw).astype(bf16)
    kk_ref[1, BLOCK:, :] = jnp.where(lo, k_sw, k).astype(bf16)
    vv_ref[0, BLOCK:, :] = jnp.where(lo, v, v_sw).astype(bf16)
    vv_ref[1, BLOCK:, :] = jnp.where(lo, v_sw, v).astype(bf16)

    o3 = ATT_W + 2 * KV_W
    z = proj[:, o3:o3 + C_CONV] * jax.nn.sigmoid(proj[:, o3 + C_CONV:o3 + 2 * C_CONV])
    z_ref[CONV_HIST:, :] = z

    bias = bias_ref[...]
    col = lax.broadcasted_iota(jnp.int32, (BLOCK, 2 * BLOCK), 1)
    lo_b = lax.broadcasted_iota(jnp.int32, (BLOCK, LANES), 1) < HEAD_DIM
    for jb in range(tt // BLOCK):
        kmin = PAD_ROWS + BLOCK - (i * tt + jb * BLOCK)
        key_ok = col >= kmin
        for grp in range(N_KV):
            k2 = kk_ref[grp, jb * BLOCK:(jb + 2) * BLOCK, :]
            v2 = vv_ref[grp, jb * BLOCK:(jb + 2) * BLOCK, :]
            for cpart in range(2):
                cblk = grp * 2 + cpart
                qc = q[cblk][jb * BLOCK:(jb + 1) * BLOCK]
                outs = []
                for half in range(2):
                    qm = jnp.where(lo_b if half == 0 else jnp.logical_not(lo_b), qc, 0.0).astype(bf16)
                    s = _dot_nt(qm, k2)
                    s = jnp.where(key_ok, s + bias, NEG_BIG)
                    p = _softmax_sink(s, sink_ref[li, cblk * 2 + half])
                    outs.append(_dot(p.astype(bf16), v2))
                a_ref[jb * BLOCK:(jb + 1) * BLOCK, cblk * LANES:(cblk + 1) * LANES] = (
                    jnp.where(lo_b, outs[0], outs[1]).astype(bf16))

    n_sh = CONV_HIST + tt - SUBLANES
    for b in range(1, SUBLANES):
        zs_ref[b, 0:n_sh, :] = z_ref[pl.ds(b, n_sh), :]
    cb = cb_ref[li:li + 1, :]
    for c0 in range(0, tt, CONV_ROWS):
        acc = jnp.broadcast_to(cb, (CONV_ROWS, C_CONV))
        for w, b, a8 in _conv_taps():
            rows = slice(c0 + a8, c0 + a8 + CONV_ROWS)
            src = z_ref[rows, :] if b == 0 else zs_ref[b, rows, :]
            acc = acc + src * cw_ref[w:w + 1, :]
        y_ref[c0:c0 + CONV_ROWS, :] = acc
    c = _layernorm_silu(y_ref[...], lng_ref[li:li + 1, :], lnb_ref[li:li + 1, :])

    out = (_dot(a_ref[...], wout_ref[0:ATT_W, :].astype(bf16))
           + _dot(c.astype(bf16), wout_ref[ATT_W:, :].astype(bf16)))
    row = i * tt + lax.broadcasted_iota(jnp.int32, (tt, 1), 0)
    o_ref[...] = jnp.where(row >= PAD_ROWS, x + out, 0.0)

    z_ref[0:CONV_HIST, :] = z_ref[tt:tt + CONV_HIST, :]
    kk_ref[:, 0:BLOCK, :] = kk_ref[:, tt:tt + BLOCK, :]
    vv_ref[:, 0:BLOCK, :] = vv_ref[:, tt:tt + BLOCK, :]

    @pl.when(i == pl.num_programs(1) - 1)
    def _():
        kst_ref[0] = k[tt - BLOCK:]
        vst_ref[0] = v[tt - BLOCK:]
        cst_ref[0] = z_ref[pl.ds(tt + CONV_HIST - (CONV_W - 1), CONV_W - 1), :]


def _ab_prompt(x, li, layer, sink, norm_g, w_in, w_out, cw, cb, lng, lnb, cos_t, sin_t, bias):
    steps = ROWS_B // TT_AB
    return pl.pallas_call(
        functools.partial(_ab_prompt_kernel, li),
        out_shape=(
            jax.ShapeDtypeStruct((R_ALL, D_MODEL), f32),
            jax.ShapeDtypeStruct((BATCH, CONV_W - 1, C_CONV), f32),
            jax.ShapeDtypeStruct((BATCH, BLOCK, KV_W), f32),
            jax.ShapeDtypeStruct((BATCH, BLOCK, KV_W), f32),
        ),
        grid=(BATCH, steps),
        in_specs=[
            pl.BlockSpec(memory_space=pltpu.SMEM),
            pl.BlockSpec((TT_AB, D_MODEL), lambda b, i: (b * steps + i, 0)),
            _layer((3, D_MODEL), layer),
            _layer((D_MODEL, IN_AB), li),
            _layer((ATT_W + C_CONV, D_MODEL), li),
            _layer((CONV_W, C_CONV), li),
            _resident((N_EVEN, C_CONV)),
            _resident((N_EVEN, C_CONV)),
            _resident((N_EVEN, C_CONV)),
            pl.BlockSpec((TT_AB, LANES), lambda b, i: (i, 0)),
            pl.BlockSpec((TT_AB, LANES), lambda b, i: (i, 0)),
            _resident((BLOCK, 2 * BLOCK)),
        ],
        out_specs=(
            pl.BlockSpec((TT_AB, D_MODEL), lambda b, i: (b * steps + i, 0)),
            pl.BlockSpec((1, CONV_W - 1, C_CONV), lambda b, i: (b, 0, 0)),
            pl.BlockSpec((1, BLOCK, KV_W), lambda b, i: (b, 0, 0)),
            pl.BlockSpec((1, BLOCK, KV_W), lambda b, i: (b, 0, 0)),
        ),
        scratch_shapes=[
            pltpu.VMEM((N_KV, BLOCK + TT_AB, LANES), bf16),
            pltpu.VMEM((N_KV, BLOCK + TT_AB, LANES), bf16),
            pltpu.VMEM((CONV_HIST + TT_AB, C_CONV), f32),
            pltpu.VMEM((SUBLANES, CONV_HIST + TT_AB, C_CONV), f32),
            pltpu.VMEM((TT_AB, C_CONV), f32),
            pltpu.VMEM((TT_AB, ATT_W), bf16),
        ],
        input_output_aliases={1: 0},
        compiler_params=_cparams(2),
        name="ab_prompt",
    )(sink, x, norm_g, w_in, w_out, cw, cb, lng, lnb, cos_t, sin_t, bias)


def _ab_sample_kernel(li, sink_ref, x_ref, g_ref, win_ref, wout_ref, cw_ref, cb_ref, lng_ref, lnb_ref, cos_ref, sin_ref,
                      ctx_ref, kc_ref, vc_ref, o_ref, cso_ref, ko_ref, vo_ref, q_scr, k_scr, v_scr, z_scr, y_scr, a_scr):
    sb = SB_SAMPLE
    heads_per_kv = N_HEADS // N_KV
    x = x_ref[...]
    h = _rms(x, g_ref[1:2, :]).astype(bf16)
    proj = _dot(h, win_ref[...].astype(bf16))

    cos_p = cos_ref[...]
    sin_p = sin_ref[...]
    lane = lax.broadcasted_iota(jnp.int32, (sb, LANES), 1)
    first = (lane % HEAD_DIM) < (ROPE_DIM // 2)
    lo = lane < HEAD_DIM
    for cblk in range(ATT_W // LANES):
        qc = _rope(proj[:, cblk * LANES:(cblk + 1) * LANES], cos_p, sin_p, first) * SCALE
        qc_sw = pltpu.roll(qc, HEAD_DIM, 1)
        for half in range(2):
            hd = 2 * cblk + half
            grp = hd // heads_per_kv
            src = qc if half == grp else qc_sw
            q_scr[:, hd * LANES:(hd + 1) * LANES] = jnp.where(lo if grp == 0 else jnp.logical_not(lo), src, 0.0)
    k_scr[...] = _rope(proj[:, ATT_W:ATT_W + KV_W], cos_p, sin_p, first)
    v_scr[...] = proj[:, ATT_W + KV_W:ATT_W + 2 * KV_W]
    o3 = ATT_W + 2 * KV_W
    z = proj[:, o3:o3 + C_CONV] * jax.nn.sigmoid(proj[:, o3 + C_CONV:o3 + 2 * C_CONV])
    z_scr[...] = z

    sub = lax.broadcasted_iota(jnp.int32, (N_HEADS, LANES), 0)
    sink_col = jnp.zeros((N_HEADS, 1), f32)
    sub1 = lax.broadcasted_iota(jnp.int32, (N_HEADS, 1), 0)
    for hd in range(N_HEADS):
        sink_col = jnp.where(sub1 == hd, sink_ref[li, hd], sink_col)
    cw_ctx = cw_ref[0:CONV_W - 1, :]
    cw_last = cw_ref[CONV_W - 1:CONV_W, :]

    def per_seq(b, carry):
        qrow = q_scr[pl.ds(b, 1), :]
        lhs = jnp.zeros((N_HEADS, LANES), f32)
        for hd in range(N_HEADS):
            lhs = jnp.where(sub == hd, qrow[:, hd * LANES:(hd + 1) * LANES], lhs)
        k_new = k_scr[pl.ds(b, 1), :]
        v_new = v_scr[pl.ds(b, 1), :]
        kc = kc_ref[b]
        vc = vc_ref[b]
        s = _dot_nt(lhs.astype(bf16), kc.astype(bf16))
        s_new = jnp.sum(lhs * k_new, axis=-1, keepdims=True)
        m = jnp.maximum(jnp.maximum(jnp.max(s, axis=-1, keepdims=True), s_new), sink_col)
        e = jnp.exp(s - m)
        e_new = jnp.exp(s_new - m)
        den = jnp.sum(e, axis=-1, keepdims=True) + e_new + jnp.exp(sink_col - m)
        o = _dot((e / den).astype(bf16), vc.astype(bf16)) + (e_new / den) * v_new
        a_scr[pl.ds(pl.multiple_of(b * N_HEADS, SUBLANES), N_HEADS), :] = o
        ko_ref[b, 0:WINDOW - 1, :] = kc_ref[b, 1:WINDOW, :]
        ko_ref[b, WINDOW - 1:WINDOW, :] = k_new
        vo_ref[b, 0:WINDOW - 1, :] = vc_ref[b, 1:WINDOW, :]
        vo_ref[b, WINDOW - 1:WINDOW, :] = v_new
        z_row = z_scr[pl.ds(b, 1), :]
        ctx = ctx_ref[b]
        y_row = jnp.sum(ctx * cw_ctx, axis=0, keepdims=True) + z_row * cw_last
        for ks in range(C_CONV // LANES):
            y_scr[pl.ds(pl.multiple_of(ks * sb * SUBLANES + b * SUBLANES, SUBLANES), SUBLANES), :] = (
                jnp.broadcast_to(y_row[:, ks * LANES:(ks + 1) * LANES], (SUBLANES, LANES)))
        cso_ref[b, 0:CONV_W - 2, :] = ctx_ref[b, 1:CONV_W - 1, :]
        cso_ref[b, CONV_W - 2:CONV_W - 1, :] = z_row
        return carry

    lax.fori_loop(0, sb, per_seq, 0)

    y = jnp.concatenate([y_scr[pl.ds(ks * sb * SUBLANES, sb, stride=SUBLANES), :] for ks in range(C_CONV // LANES)],
                        axis=1)
    blocks = []
    for cblk in range(ATT_W // LANES):
        parts = []
        for half in range(2):
            hd = 2 * cblk + half
            o_h = a_scr[pl.ds(hd, sb, stride=N_HEADS), :]
            parts.append(o_h if half == hd // heads_per_kv else pltpu.roll(o_h, HEAD_DIM, 1))
        blocks.append(jnp.where(lo, parts[0], parts[1]))
    a = jnp.concatenate(blocks, axis=1)
    c = _layernorm_silu(y + cb_ref[li:li + 1, :], lng_ref[li:li + 1, :], lnb_ref[li:li + 1, :])
    out = (_dot(a.astype(bf16), wout_ref[0:ATT_W, :].astype(bf16))
           + _dot(c.astype(bf16), wout_ref[ATT_W:, :].astype(bf16)))
    o_ref[...] = x + out


def _ab_sample(x, li, layer, sink, norm_g, w_in, w_out, cw, cb, lng, lnb, cos_s, sin_s, ctx, kc, vc):
    steps = DEC_BATCH // SB_SAMPLE
    first_blk = R_PROMPT // SB_SAMPLE
    seq3 = lambda i: (i, 0, 0)
    return pl.pallas_call(
        functools.partial(_ab_sample_kernel, li),
        out_shape=(
            jax.ShapeDtypeStruct((R_ALL, D_MODEL), f32),
            jax.ShapeDtypeStruct((DEC_BATCH, CONV_W - 1, C_CONV), f32),
            jax.ShapeDtypeStruct((DEC_BATCH, WINDOW, KV_W), f32),
            jax.ShapeDtypeStruct((DEC_BATCH, WINDOW, KV_W), f32),
        ),
        grid=(steps,),
        in_specs=[
            pl.BlockSpec(memory_space=pltpu.SMEM),
            pl.BlockSpec((SB_SAMPLE, D_MODEL), lambda i: (first_blk + i, 0)),
            _layer((3, D_MODEL), layer),
            _layer((D_MODEL, IN_AB), li),
            _layer((ATT_W + C_CONV, D_MODEL), li),
            _layer((CONV_W, C_CONV), li),
            _resident((N_EVEN, C_CONV)),
            _resident((N_EVEN, C_CONV)),
            _resident((N_EVEN, C_CONV)),
            _resident((1, LANES)),
            _resident((1, LANES)),
            pl.BlockSpec((None, SB_SAMPLE, CONV_W - 1, C_CONV), lambda i: (li, i, 0, 0)),
            pl.BlockSpec((SB_SAMPLE, WINDOW, KV_W), seq3),
            pl.BlockSpec((SB_SAMPLE, WINDOW, KV_W), seq3),
        ],
        out_specs=(
            pl.BlockSpec((SB_SAMPLE, D_MODEL), lambda i: (first_blk + i, 0)),
            pl.BlockSpec((SB_SAMPLE, CONV_W - 1, C_CONV), seq3),
            pl.BlockSpec((SB_SAMPLE, WINDOW, KV_W), seq3),
            pl.BlockSpec((SB_SAMPLE, WINDOW, KV_W), seq3),
        ),
        scratch_shapes=[
            pltpu.VMEM((SB_SAMPLE, N_HEADS * LANES), f32),
            pltpu.VMEM((SB_SAMPLE, KV_W), f32),
            pltpu.VMEM((SB_SAMPLE, KV_W), f32),
            pltpu.VMEM((SB_SAMPLE, C_CONV), f32),
            pltpu.VMEM((C_CONV // LANES * SB_SAMPLE * SUBLANES, LANES), f32),
            pltpu.VMEM((SB_SAMPLE * N_HEADS, LANES), f32),
        ],
        input_output_aliases={1: 0},
        compiler_params=_cparams(1),
        name="ab_sample",
    )(sink, x, norm_g, w_in, w_out, cw, cb, lng, lnb, cos_s, sin_s, ctx, kc, vc)


def _ssm_prep_kernel(are_ref, aim_ref, ldt_ref, bre_ref, bim_ref, lre_ref, lim_ref, pre_ref, pim_ref, bbre_ref, bbim_ref):
    a_re = are_ref[...]
    a_im = aim_ref[...]
    dt = jnp.exp(ldt_ref[...])
    mag = jnp.exp(a_re * dt)
    ang = a_im * dt
    lam_re = mag * jnp.cos(ang)
    lam_im = mag * jnp.sin(ang)
    den = a_re * a_re + a_im * a_im
    nr = lam_re - 1.0
    f_re = (nr * a_re + lam_im * a_im) / den
    f_im = (lam_im * a_re - nr * a_im) / den
    b_re = bre_ref[...]
    b_im = bim_ref[...]
    bbre_ref[...] = f_re * b_re - f_im * b_im
    bbim_ref[...] = f_re * b_im + f_im * b_re
    lre_ref[...] = lam_re
    lim_ref[...] = lam_im
    p_re, p_im = lam_re, lam_im
    r_re = r_im = None
    n = SEG_LEN
    while n:
        if n & 1:
            if r_re is None:
                r_re, r_im = p_re, p_im
            else:
                r_re, r_im = r_re * p_re - r_im * p_im, r_re * p_im + r_im * p_re
        n >>= 1
        if n:
            p_re, p_im = p_re * p_re - p_im * p_im, 2.0 * p_re * p_im
    pre_ref[...] = r_re
    pim_ref[...] = r_im


def _ssm_prep(a_re, a_im, log_dt, b_re, b_im, c_re, c_im):
    flat = lambda t: t.reshape(1, N_CPLX)
    ldt = jnp.repeat(log_dt, SSM_STATE).reshape(1, N_CPLX)
    b_t = lambda t: t.transpose(2, 0, 1).reshape(SSM_GROUP, N_CPLX)
    row = jax.ShapeDtypeStruct((1, N_CPLX), f32)
    mat = jax.ShapeDtypeStruct((SSM_GROUP, N_CPLX), f32)
    lre, lim, pre, pim, bbre, bbim = pl.pallas_call(
        _ssm_prep_kernel, out_shape=(row, row, row, row, mat, mat), name="ssm_prep",
    )(flat(a_re), flat(a_im), ldt, b_t(b_re), b_t(b_im))
    eye = jnp.eye(SSM_GROUP, dtype=f32)
    gq = N_GROUPS // N_QUART

    def in_mat(bb):
        t = bb.reshape(SSM_GROUP, N_QUART, gq, SSM_STATE).transpose(1, 2, 0, 3)
        return (t[:, :, :, None, :] * eye[None, :, None, :, None]).reshape(N_QUART, Q_CH, Q_ST).astype(bf16)

    def out_mat(cc):
        t = cc.reshape(N_QUART, gq, SSM_GROUP, SSM_STATE).transpose(0, 1, 3, 2)
        return (t[:, :, :, None, :] * eye[None, :, None, :, None]).reshape(N_QUART, Q_ST, Q_CH).astype(bf16)

    return lre, lim, pre, pim, in_mat(bbre), in_mat(bbim), out_mat(c_re), out_mat(c_im)


def _cmul_add(lr, li, hr, hi, xr, xi):
    return lr * hr - li * hi + xr, lr * hi + li * hr + xi


def _static_loop(n, body, carry):
    return lax.fori_loop(0, n, body, carry)


def _ssm_prompt_kernel(lidx, x_ref, g_ref, win_ref, bbre_ref, bbim_ref, ccre_ref, ccim_ref, lre_ref, lim_ref, pre_ref,
                       pim_ref, d_ref, wglu_ref, o_ref, sre_ref, sim_ref, slab_ref, perm_ref, xr_ref, xi_ref, y_ref,
                       cr_ref, ci_ref):
    tt = TT_SSM
    i = pl.program_id(1)

    @pl.when(i == 0)
    def _():
        cr_ref[...] = jnp.zeros_like(cr_ref)
        ci_ref[...] = jnp.zeros_like(ci_ref)

    x = x_ref[...]
    h = _rms(x, g_ref[1:2, :]).astype(bf16)
    u = _dot(h, win_ref[...].astype(bf16))

    n_slab = D_MODEL // LANES
    for ks in range(n_slab):
        slab_ref[ks * tt:(ks + 1) * tt, :] = u[:, ks * LANES:(ks + 1) * LANES]

    def gather(t, carry):
        for ks in range(n_slab):
            perm_ref[pl.ds(t * SCAN_SEGS, SCAN_SEGS), ks * LANES:(ks + 1) * LANES] = (
                slab_ref[pl.ds(ks * tt + t, SCAN_SEGS, stride=SEG_LEN), :])
        return carry

    _static_loop(SEG_LEN, gather, 0)
    up = perm_ref[...]
    ub = up.astype(bf16)

    for qd in range(N_QUART):
        st = slice(qd * Q_ST, (qd + 1) * Q_ST)
        ubq = ub[:, qd * Q_CH:(qd + 1) * Q_CH]
        xr_ref[...] = _dot(ubq, bbre_ref[qd])
        xi_ref[...] = _dot(ubq, bbim_ref[qd])
        lr = jnp.broadcast_to(lre_ref[:, st], (SCAN_SEGS, Q_ST))
        li = jnp.broadcast_to(lim_ref[:, st], (SCAN_SEGS, Q_ST))

        def local_end(t, hc):
            rows = pl.ds(t * SCAN_SEGS, SCAN_SEGS)
            return _cmul_add(lr, li, hc[0], hc[1], xr_ref[rows, :], xi_ref[rows, :])

        zero = jnp.zeros((SCAN_SEGS, Q_ST), f32)
        er, ei = _static_loop(SEG_LEN, local_end, (zero, zero))

        pr = pre_ref[:, st]
        pi = pim_ref[:, st]
        c_r = cr_ref[:, st]
        c_i = ci_ref[:, st]
        rows_r, rows_i = [c_r], [c_i]
        for j in range(1, SCAN_SEGS):
            c_r, c_i = _cmul_add(pr, pi, c_r, c_i, er[j - 1:j], ei[j - 1:j])
            rows_r.append(c_r)
            rows_i.append(c_i)
        c_r, c_i = _cmul_add(pr, pi, c_r, c_i, er[SCAN_SEGS - 1:], ei[SCAN_SEGS - 1:])
        cr_ref[:, st] = c_r
        ci_ref[:, st] = c_i
        start = (jnp.concatenate(rows_r, axis=0), jnp.concatenate(rows_i, axis=0))

        def scan(t, hc):
            rows = pl.ds(t * SCAN_SEGS, SCAN_SEGS)
            hr, hi = _cmul_add(lr, li, hc[0], hc[1], xr_ref[rows, :], xi_ref[rows, :])
            xr_ref[rows, :] = hr
            xi_ref[rows, :] = hi
            return hr, hi

        _static_loop(SEG_LEN, scan, start)
        y_ref[:, qd * Q_CH:(qd + 1) * Q_CH] = (_dot(xr_ref[...].astype(bf16), ccre_ref[qd])
                                               - _dot(xi_ref[...].astype(bf16), ccim_ref[qd]))

    y = y_ref[...] + d_ref[lidx:lidx + 1, :] * up
    vg = _dot(jax.nn.gelu(y).astype(bf16), wglu_ref[...].astype(bf16))
    perm_ref[...] = vg[:, :D_MODEL] * jax.nn.sigmoid(vg[:, D_MODEL:])

    def scatter(t, carry):
        for ks in range(n_slab):
            slab_ref[pl.ds(ks * tt + t, SCAN_SEGS, stride=SEG_LEN), :] = (
                perm_ref[pl.ds(t * SCAN_SEGS, SCAN_SEGS), ks * LANES:(ks + 1) * LANES])
        return carry

    _static_loop(SEG_LEN, scatter, 0)
    out = jnp.concatenate([slab_ref[ks * tt:(ks + 1) * tt, :] for ks in range(n_slab)], axis=1)
    row = i * tt + lax.broadcasted_iota(jnp.int32, (tt, 1), 0)
    o_ref[...] = jnp.where(row >= PAD_ROWS, x + out, 0.0)

    @pl.when(i == pl.num_programs(1) - 1)
    def _():
        sre_ref[0] = cr_ref[...]
        sim_ref[0] = ci_ref[...]


def _ssm_prompt(x, li, layer, norm_g, w_in, bbre, bbim, ccre, ccim, lre, lim, pre, pim, d, w_glu):
    steps = ROWS_B // TT_SSM
    state = jax.ShapeDtypeStruct((BATCH, 1, N_CPLX), f32)
    return pl.pallas_call(
        functools.partial(_ssm_prompt_kernel, li),
        out_shape=(jax.ShapeDtypeStruct((R_ALL, D_MODEL), f32), state, state),
        grid=(BATCH, steps),
        in_specs=[
            pl.BlockSpec((TT_SSM, D_MODEL), lambda b, i: (b * steps + i, 0)),
            _layer((3, D_MODEL), layer),
            _layer((D_MODEL, D_MODEL), li),
            _resident((N_QUART, Q_CH, Q_ST)),
            _resident((N_QUART, Q_CH, Q_ST)),
            _resident((N_QUART, Q_ST, Q_CH)),
            _resident((N_QUART, Q_ST, Q_CH)),
            _resident((1, N_CPLX)),
            _resident((1, N_CPLX)),
            _resident((1, N_CPLX)),
            _resident((1, N_CPLX)),
            _resident((N_ODD, D_MODEL)),
            _layer((D_MODEL, 2 * D_MODEL), li),
        ],
        out_specs=(
            pl.BlockSpec((TT_SSM, D_MODEL), lambda b, i: (b * steps + i, 0)),
            pl.BlockSpec((1, 1, N_CPLX), lambda b, i: (b, 0, 0)),
            pl.BlockSpec((1, 1, N_CPLX), lambda b, i: (b, 0, 0)),
        ),
        scratch_shapes=[
            pltpu.VMEM((D_MODEL // LANES * TT_SSM, LANES), f32),
            pltpu.VMEM((TT_SSM, D_MODEL), f32),
            pltpu.VMEM((TT_SSM, Q_ST), f32),
            pltpu.VMEM((TT_SSM, Q_ST), f32),
            pltpu.VMEM((TT_SSM, D_MODEL), f32),
            pltpu.VMEM((1, N_CPLX), f32),
            pltpu.VMEM((1, N_CPLX), f32),
        ],
        input_output_aliases={0: 0},
        compiler_params=_cparams(2),
        name="ssm_prompt",
    )(x, norm_g, w_in, bbre, bbim, ccre, ccim, lre, lim, pre, pim, d, w_glu)


def _ssm_sample_kernel(li, x_ref, g_ref, win_ref, bbre_ref, bbim_ref, ccre_ref, ccim_ref, lre_ref, lim_ref, d_ref,
                       wglu_ref, sre_in, sim_in, o_ref, sre_ref, sim_ref, y_ref):
    x = x_ref[...]
    h = _rms(x, g_ref[1:2, :]).astype(bf16)
    u = _dot(h, win_ref[...].astype(bf16))
    ub = u.astype(bf16)
    for qd in range(N_QUART):
        st = slice(qd * Q_ST, (qd + 1) * Q_ST)
        ubq = ub[:, qd * Q_CH:(qd + 1) * Q_CH]
        hr, hi = _cmul_add(lre_ref[:, st], lim_ref[:, st], sre_in[:, st], sim_in[:, st],
                           _dot(ubq, bbre_ref[qd]), _dot(ubq, bbim_ref[qd]))
        sre_ref[:, st] = hr
        sim_ref[:, st] = hi
        y_ref[:, qd * Q_CH:(qd + 1) * Q_CH] = _dot(hr.astype(bf16), ccre_ref[qd]) - _dot(hi.astype(bf16), ccim_ref[qd])
    y = y_ref[...] + d_ref[li:li + 1, :] * u
    vg = _dot(jax.nn.gelu(y).astype(bf16), wglu_ref[...].astype(bf16))
    o_ref[...] = x + vg[:, :D_MODEL] * jax.nn.sigmoid(vg[:, D_MODEL:])


def _ssm_sample(x, li, layer, norm_g, w_in, bbre, bbim, ccre, ccim, lre, lim, d, w_glu, s_re, s_im):
    state = jax.ShapeDtypeStruct((DEC_BATCH, N_CPLX), f32)
    blk = R_PROMPT // DEC_BATCH
    return pl.pallas_call(
        functools.partial(_ssm_sample_kernel, li),
        out_shape=(jax.ShapeDtypeStruct((R_ALL, D_MODEL), f32), state, state),
        grid=(1,),
        in_specs=[
            pl.BlockSpec((DEC_BATCH, D_MODEL), lambda i: (blk, 0)),
            _layer((3, D_MODEL), layer),
            _layer((D_MODEL, D_MODEL), li),
            _resident((N_QUART, Q_CH, Q_ST)),
            _resident((N_QUART, Q_CH, Q_ST)),
            _resident((N_QUART, Q_ST, Q_CH)),
            _resident((N_QUART, Q_ST, Q_CH)),
            _resident((1, N_CPLX)),
            _resident((1, N_CPLX)),
            _resident((N_ODD, D_MODEL)),
            _layer((D_MODEL, 2 * D_MODEL), li),
            _layer((DEC_BATCH, N_CPLX), li),
            _layer((DEC_BATCH, N_CPLX), li),
        ],
        out_specs=(
            pl.BlockSpec((DEC_BATCH, D_MODEL), lambda i: (blk, 0)),
            pl.BlockSpec((DEC_BATCH, N_CPLX), lambda i: (0, 0)),
            pl.BlockSpec((DEC_BATCH, N_CPLX), lambda i: (0, 0)),
        ),
        scratch_shapes=[pltpu.VMEM((DEC_BATCH, D_MODEL), f32)],
        input_output_aliases={0: 0},
        compiler_params=_cparams(1),
        name="ssm_sample",
    )(x, norm_g, w_in, bbre, bbim, ccre, ccim, lre, lim, d, w_glu, s_re, s_im)


def _rope_tables():
    half = ROPE_DIM // 2
    inv = ROPE_THETA ** (-jnp.arange(half, dtype=f32) * 2.0 / ROPE_DIM)
    pos = jnp.concatenate([jnp.arange(ROWS_B, dtype=jnp.int32) - PAD_ROWS, jnp.full((1,), PAST_LEN, jnp.int32)])
    ang = pos.astype(f32)[:, None] * inv[None, :]
    cos, sin = jnp.cos(ang), jnp.sin(ang)
    ones = jnp.ones((pos.shape[0], HEAD_DIM - ROPE_DIM), f32)
    cos_h = jnp.concatenate([cos, cos, ones], axis=1)
    sin_h = jnp.concatenate([-sin, sin, 0.0 * ones], axis=1)
    cos_t = jnp.concatenate([cos_h, cos_h], axis=1)
    sin_t = jnp.concatenate([sin_h, sin_h], axis=1)
    return cos_t[:ROWS_B], sin_t[:ROWS_B], cos_t[ROWS_B:], sin_t[ROWS_B:]


def _window_bias():
    r = jnp.arange(BLOCK, dtype=jnp.int32)[:, None]
    s = jnp.arange(2 * BLOCK, dtype=jnp.int32)[None, :]
    return jnp.where((s >= r) & (s <= r + WINDOW), 0.0, NEG_BIG).astype(f32)


def kernel(x_prompt, x_sample, state_conv, cache_win_k, cache_win_v, state_ssm_re, state_ssm_im, meta_tokens, norm_g, final_norm_g, ffn1_w_gu, ffn1_w_down, ffn2_w_gu, ffn2_w_down, ab_w_in, ab_w_out, attn_sink, conv_w, conv_b, conv_ln_g, conv_ln_b, ssm_w_in, ssm_a_re, ssm_a_im, ssm_log_dt, ssm_b_re, ssm_b_im, ssm_c_re, ssm_c_im, ssm_d, ssm_w_glu):
    meta = jnp.broadcast_to(meta_tokens[None], (BATCH, N_META, D_MODEL))
    pad = jnp.zeros((BATCH, PAD_ROWS, D_MODEL), f32)
    x = jnp.concatenate([jnp.concatenate([pad, meta, x_prompt], axis=1).reshape(R_PROMPT, D_MODEL),
                         x_sample.reshape(DEC_BATCH, D_MODEL)], axis=0)

    cos_t, sin_t, cos_s, sin_s = _rope_tables()
    bias = _window_bias()
    row = lambda t: t.reshape(1, -1)

    p_conv, p_k, p_v, p_re, p_im = [], [], [], [], []
    s_conv, s_k, s_v, s_re, s_im = [], [], [], [], []
    s_re_in = state_ssm_re.reshape(N_ODD, DEC_BATCH, N_CPLX)
    s_im_in = state_ssm_im.reshape(N_ODD, DEC_BATCH, N_CPLX)
    for l in range(DEPTH):
        x = _ffn(x, norm_g, ffn1_w_gu, ffn1_w_down, l, 0)
        i = l // 2
        if l % 2 == 0:
            wargs = (attn_sink, norm_g, ab_w_in, ab_w_out, conv_w, conv_b, conv_ln_g, conv_ln_b)
            x, cst, kst, vst = _ab_prompt(x, i, l, *wargs, cos_t, sin_t, bias)
            x, cso, kso, vso = _ab_sample(x, i, l, *wargs, cos_s, sin_s, state_conv,
                                          cache_win_k[i].reshape(DEC_BATCH, WINDOW, KV_W),
                                          cache_win_v[i].reshape(DEC_BATCH, WINDOW, KV_W))
            p_conv.append(cst)
            p_k.append(kst.reshape(BATCH, WINDOW, N_KV, HEAD_DIM))
            p_v.append(vst.reshape(BATCH, WINDOW, N_KV, HEAD_DIM))
            s_conv.append(cso)
            s_k.append(kso.reshape(DEC_BATCH, WINDOW, N_KV, HEAD_DIM))
            s_v.append(vso.reshape(DEC_BATCH, WINDOW, N_KV, HEAD_DIM))
        else:
            lre, lim, pre, pim, bbre, bbim, ccre, ccim = _ssm_prep(
                ssm_a_re[i], ssm_a_im[i], ssm_log_dt[i], ssm_b_re[i], ssm_b_im[i], ssm_c_re[i], ssm_c_im[i])
            x, pre_s, pim_s = _ssm_prompt(x, i, l, norm_g, ssm_w_in, bbre, bbim, ccre, ccim, lre, lim, pre, pim,
                                          ssm_d, ssm_w_glu)
            x, sre_s, sim_s = _ssm_sample(x, i, l, norm_g, ssm_w_in, bbre, bbim, ccre, ccim, lre, lim,
                                          ssm_d, ssm_w_glu, s_re_in, s_im_in)
            p_re.append(pre_s.reshape(BATCH, N_GROUPS, SSM_STATE))
            p_im.append(pim_s.reshape(BATCH, N_GROUPS, SSM_STATE))
            s_re.append(sre_s.reshape(DEC_BATCH, N_GROUPS, SSM_STATE))
            s_im.append(sim_s.reshape(DEC_BATCH, N_GROUPS, SSM_STATE))
        x = _ffn(x, norm_g, ffn2_w_gu, ffn2_w_down, l, 2)

    y_prompt, y_sample = _final_norm(x, row(final_norm_g))
    return (y_prompt, y_sample,
            jnp.stack(p_conv), jnp.stack(p_k), jnp.stack(p_v), jnp.stack(p_re), jnp.stack(p_im),
            jnp.stack(s_conv), jnp.stack(s_k), jnp.stack(s_v), jnp.stack(s_re), jnp.stack(s_im))
```

```python
import functools
import math

import jax
import jax.numpy as jnp
from jax import lax
from jax.experimental import pallas as pl
from jax.experimental.pallas import tpu as pltpu

f32 = jnp.float32
bf16 = jnp.bfloat16

D_MODEL = 1024
BATCH = 4
SEQ = 4096
DEPTH = 4
DEC_BATCH = 128
PAST_LEN = 8192
N_META = 16
EPS = 1e-6
D_FF = 2816
N_HEADS = 8
N_KV = 2
HEAD_DIM = 64
WINDOW = 128
BLOCK = 128
ROPE_DIM = HEAD_DIM // 4
ROPE_THETA = 500000.0
SCALE = HEAD_DIM ** -0.5
C_CONV = 512
CONV_W = 31
SSM_GROUP = 16
N_GROUPS = D_MODEL // SSM_GROUP
SSM_STATE = 64
ATT_W = N_HEADS * HEAD_DIM
KV_W = N_KV * HEAD_DIM
IN_AB = ATT_W + 2 * KV_W + 2 * C_CONV
N_CPLX = N_GROUPS * SSM_STATE
N_EVEN = (DEPTH + 1) // 2
N_ODD = DEPTH // 2

LANES = 128
SUBLANES = 8
VMEM_LIMIT = 56 * 1024 * 1024

L_PROMPT = N_META + SEQ
PAD_ROWS = (-L_PROMPT) % BLOCK
ROWS_B = PAD_ROWS + L_PROMPT
R_PROMPT = BATCH * ROWS_B
R_ALL = R_PROMPT + DEC_BATCH

TM_FFN = 448
TT_AB = 384
CONV_HIST = 32
CONV_ROWS = 64
SCAN_SEGS = SUBLANES
SEG_LEN = 66
TT_SSM = SCAN_SEGS * SEG_LEN
N_QUART = 4
Q_CH = D_MODEL // N_QUART
Q_ST = N_CPLX // N_QUART
NORM_BLOCKS = 8
SB_SAMPLE = 32
NEG_BIG = -1e30


def _cparams(n_axes):
    return pltpu.CompilerParams(dimension_semantics=("arbitrary",) * n_axes, vmem_limit_bytes=VMEM_LIMIT)


def _resident(shape):
    zeros = (0,) * len(shape)
    return pl.BlockSpec(shape, lambda *_: zeros, pipeline_mode=pl.Buffered(1))


def _layer(shape, layer):
    idx = (layer,) + (0,) * len(shape)
    return pl.BlockSpec((None,) + tuple(shape), lambda *_: idx, pipeline_mode=pl.Buffered(1))


def _rms(x, g):
    ms = jnp.mean(x * x, axis=-1, keepdims=True)
    return x * lax.rsqrt(ms + EPS) * g


def _dot(a, b):
    return jnp.dot(a, b, preferred_element_type=f32)


def _dot_nt(a, b):
    return lax.dot_general(a, b, (((1,), (1,)), ((), ())), preferred_element_type=f32)


def _ffn_kernel(norm_row, x_ref, g_ref, wgu_ref, wd_ref, o_ref):
    x = x_ref[...]
    h = _rms(x, g_ref[norm_row:norm_row + 1, :]).astype(bf16)
    gate = _dot(h, wgu_ref[:, :D_FF].astype(bf16))
    up = _dot(h, wgu_ref[:, D_FF:].astype(bf16))
    act = (gate * jax.nn.sigmoid(gate) * up).astype(bf16)
    o_ref[...] = x + 0.5 * _dot(act, wd_ref[...].astype(bf16))


def _ffn(x, norm_g, w_gu, w_down, layer, norm_row):
    return pl.pallas_call(
        functools.partial(_ffn_kernel, norm_row),
        out_shape=jax.ShapeDtypeStruct((R_ALL, D_MODEL), f32),
        grid=(R_ALL // TM_FFN,),
        in_specs=[
            pl.BlockSpec((TM_FFN, D_MODEL), lambda i: (i, 0)),
            _layer((3, D_MODEL), layer),
            _layer((D_MODEL, 2 * D_FF), layer),
            _layer((D_FF, D_MODEL), layer),
        ],
        out_specs=pl.BlockSpec((TM_FFN, D_MODEL), lambda i: (i, 0)),
        input_output_aliases={0: 0},
        compiler_params=_cparams(1),
        name="ffn",
    )(x, norm_g, w_gu, w_down)


def _norm_kernel(x_ref, g_ref, o_ref):
    o_ref[...] = _rms(x_ref[...], g_ref[...]).reshape(o_ref.shape)


def _final_norm(x, g):
    blocks_b = ROWS_B // BLOCK
    y_prompt = pl.pallas_call(
        _norm_kernel,
        out_shape=jax.ShapeDtypeStruct((BATCH, SEQ, D_MODEL), f32),
        grid=(BATCH, SEQ // BLOCK),
        in_specs=[pl.BlockSpec((BLOCK, D_MODEL), lambda b, j: (b * blocks_b + 1 + j, 0)), _resident((1, D_MODEL))],
        out_specs=pl.BlockSpec((1, BLOCK, D_MODEL), lambda b, j: (b, j, 0)),
        compiler_params=_cparams(2),
        name="final_norm_prompt",
    )(x, g)
    y_sample = pl.pallas_call(
        _norm_kernel,
        out_shape=jax.ShapeDtypeStruct((DEC_BATCH, D_MODEL), f32),
        grid=(1,),
        in_specs=[pl.BlockSpec((DEC_BATCH, D_MODEL), lambda i: (R_PROMPT // DEC_BATCH, 0)), _resident((1, D_MODEL))],
        out_specs=pl.BlockSpec((DEC_BATCH, D_MODEL), lambda i: (0, 0)),
        compiler_params=_cparams(1),
        name="final_norm_sample",
    )(x, g)
    return y_prompt, y_sample.reshape(DEC_BATCH, 1, D_MODEL)


def _rope(t, cos_p, sin_p, first):
    half = ROPE_DIM // 2
    partner = jnp.where(first, pltpu.roll(t, LANES - half, 1), pltpu.roll(t, half, 1))
    return t * cos_p + partner * sin_p


def _layernorm_silu(y, g, b):
    mu = jnp.mean(y, axis=-1, keepdims=True)
    yc = y - mu
    yn = yc * lax.rsqrt(jnp.mean(yc * yc, axis=-1, keepdims=True) + EPS) * g + b
    return yn * jax.nn.sigmoid(yn)


def _softmax_sink(s, sink):
    m = jnp.maximum(jnp.max(s, axis=-1, keepdims=True), sink)
    e = jnp.exp(s - m)
    den = jnp.sum(e, axis=-1, keepdims=True) + jnp.exp(sink - m)
    return e / den


def _conv_taps():
    taps = []
    for w in range(CONV_W):
        idx = CONV_HIST - (CONV_W - 1) + w
        taps.append((w, idx % SUBLANES, idx - idx % SUBLANES))
    return taps


def _ab_prompt_kernel(li, sink_ref, x_ref, g_ref, win_ref, wout_ref, cw_ref, cb_ref, lng_ref, lnb_ref, cos_ref, sin_ref,
                      bias_ref, o_ref, cst_ref, kst_ref, vst_ref, kk_ref, vv_ref, z_ref, zs_ref, y_ref, a_ref):
    tt = TT_AB
    i = pl.program_id(1)

    @pl.when(i == 0)
    def _():
        kk_ref[:, 0:BLOCK, :] = jnp.zeros((N_KV, BLOCK, LANES), bf16)
        vv_ref[:, 0:BLOCK, :] = jnp.zeros((N_KV, BLOCK, LANES), bf16)
        z_ref[0:CONV_HIST, :] = jnp.zeros((CONV_HIST, C_CONV), f32)

    x = x_ref[...]
    h = _rms(x, g_ref[1:2, :]).astype(bf16)
    proj = _dot(h, win_ref[...].astype(bf16))

    cos_p = cos_ref[...]
    sin_p = sin_ref[...]
    lane = lax.broadcasted_iota(jnp.int32, (tt, LANES), 1)
    first = (lane % HEAD_DIM) < (ROPE_DIM // 2)
    lo = lane < HEAD_DIM

    q = [_rope(proj[:, c * LANES:(c + 1) * LANES], cos_p, sin_p, first) * SCALE for c in range(ATT_W // LANES)]
    k = _rope(proj[:, ATT_W:ATT_W + KV_W], cos_p, sin_p, first)
    v = proj[:, ATT_W + KV_W:ATT_W + 2 * KV_W]
    k_sw = pltpu.roll(k, HEAD_DIM, 1)
    v_sw = pltpu.roll(v, HEAD_DIM, 1)
    kk_ref[0, BLOCK:, :] = jnp.where(lo, k, k_sw).astype(bf16)
    kk_ref[1, BLOCK:, :] = jnp.where(lo, k_sw, k).astype(bf16)
    vv_ref[0, BLOCK:, :] = jnp.where(lo, v, v_sw).astype(bf16)
    vv_ref[1, BLOCK:, :] = jnp.where(lo, v_sw, v).astype(bf16)

    o3 = ATT_W + 2 * KV_W
    z = proj[:, o3:o3 + C_CONV] * jax.nn.sigmoid(proj[:, o3 + C_CONV:o3 + 2 * C_CONV])
    z_ref[CONV_HIST:, :] = z

    bias = bias_ref[...]
    col = lax.broadcasted_iota(jnp.int32, (BLOCK, 2 * BLOCK), 1)
    lo_b = lax.broadcasted_iota(jnp.int32, (BLOCK, LANES), 1) < HEAD_DIM
    for jb in range(tt // BLOCK):
        kmin = PAD_ROWS + BLOCK - (i * tt + jb * BLOCK)
        key_ok = col >= kmin
        for grp in range(N_KV):
            k2 = kk_ref[grp, jb * BLOCK:(jb + 2) * BLOCK, :]
            v2 = vv_ref[grp, jb * BLOCK:(jb + 2) * BLOCK, :]
            for cpart in range(2):
                cblk = grp * 2 + cpart
                qc = q[cblk][jb * BLOCK:(jb + 1) * BLOCK]
                outs = []
                for half in range(2):
                    qm = jnp.where(lo_b if half == 0 else jnp.logical_not(lo_b), qc, 0.0).astype(bf16)
                    s = _dot_nt(qm, k2)
                    s = jnp.where(key_ok, s + bias, NEG_BIG)
                    p = _softmax_sink(s, sink_ref[li, cblk * 2 + half])
                    outs.append(_dot(p.astype(bf16), v2))
                a_ref[jb * BLOCK:(jb + 1) * BLOCK, cblk * LANES:(cblk + 1) * LANES] = (
                    jnp.where(lo_b, outs[0], outs[1]).astype(bf16))

    n_sh = CONV_HIST + tt - SUBLANES
    for b in range(1, SUBLANES):
        zs_ref[b, 0:n_sh, :] = z_ref[pl.ds(b, n_sh), :]
    cb = cb_ref[li:li + 1, :]
    for c0 in range(0, tt, CONV_ROWS):
        acc = jnp.broadcast_to(cb, (CONV_ROWS, C_CONV))
        for w, b, a8 in _conv_taps():
            rows = slice(c0 + a8, c0 + a8 + CONV_ROWS)
            src = z_ref[rows, :] if b == 0 else zs_ref[b, rows, :]
            acc = acc + src * cw_ref[w:w + 1, :]
        y_ref[c0:c0 + CONV_ROWS, :] = acc
    c = _layernorm_silu(y_ref[...], lng_ref[li:li + 1, :], lnb_ref[li:li + 1, :])

    out = (_dot(a_ref[...], wout_ref[0:ATT_W, :].astype(bf16))
           + _dot(c.astype(bf16), wout_ref[ATT_W:, :].astype(bf16)))
    row = i * tt + lax.broadcasted_iota(jnp.int32, (tt, 1), 0)
    o_ref[...] = jnp.where(row >= PAD_ROWS, x + out, 0.0)

    z_ref[0:CONV_HIST, :] = z_ref[tt:tt + CONV_HIST, :]
    kk_ref[:, 0:BLOCK, :] = kk_ref[:, tt:tt + BLOCK, :]
    vv_ref[:, 0:BLOCK, :] = vv_ref[:, tt:tt + BLOCK, :]

    @pl.when(i == pl.num_programs(1) - 1)
    def _():
        kst_ref[0] = k[tt - BLOCK:]
        vst_ref[0] = v[tt - BLOCK:]
        cst_ref[0] = z_ref[pl.ds(tt + CONV_HIST - (CONV_W - 1), CONV_W - 1), :]


def _ab_prompt(x, li, layer, sink, norm_g, w_in, w_out, cw, cb, lng, lnb, cos_t, sin_t, bias):
    steps = ROWS_B // TT_AB
    return pl.pallas_call(
        functools.partial(_ab_prompt_kernel, li),
        out_shape=(
            jax.ShapeDtypeStruct((R_ALL, D_MODEL), f32),
            jax.ShapeDtypeStruct((BATCH, CONV_W - 1, C_CONV), f32),
            jax.ShapeDtypeStruct((BATCH, BLOCK, KV_W), f32),
            jax.ShapeDtypeStruct((BATCH, BLOCK, KV_W), f32),
        ),
        grid=(BATCH, steps),
        in_specs=[
            pl.BlockSpec(memory_space=pltpu.SMEM),
            pl.BlockSpec((TT_AB, D_MODEL), lambda b, i: (b * steps + i, 0)),
            _layer((3, D_MODEL), layer),
            _layer((D_MODEL, IN_AB), li),
            _layer((ATT_W + C_CONV, D_MODEL), li),
            _layer((CONV_W, C_CONV), li),
            _resident((N_EVEN, C_CONV)),
            _resident((N_EVEN, C_CONV)),
            _resident((N_EVEN, C_CONV)),
            pl.BlockSpec((TT_AB, LANES), lambda b, i: (i, 0)),
            pl.BlockSpec((TT_AB, LANES), lambda b, i: (i, 0)),
            _resident((BLOCK, 2 * BLOCK)),
        ],
        out_specs=(
            pl.BlockSpec((TT_AB, D_MODEL), lambda b, i: (b * steps + i, 0)),
            pl.BlockSpec((1, CONV_W - 1, C_CONV), lambda b, i: (b, 0, 0)),
            pl.BlockSpec((1, BLOCK, KV_W), lambda b, i: (b, 0, 0)),
            pl.BlockSpec((1, BLOCK, KV_W), lambda b, i: (b, 0, 0)),
        ),
        scratch_shapes=[
            pltpu.VMEM((N_KV, BLOCK + TT_AB, LANES), bf16),
            pltpu.VMEM((N_KV, BLOCK + TT_AB, LANES), bf16),
            pltpu.VMEM((CONV_HIST + TT_AB, C_CONV), f32),
            pltpu.VMEM((SUBLANES, CONV_HIST + TT_AB, C_CONV), f32),
            pltpu.VMEM((TT_AB, C_CONV), f32),
            pltpu.VMEM((TT_AB, ATT_W), bf16),
        ],
        input_output_aliases={1: 0},
        compiler_params=_cparams(2),
        name="ab_prompt",
    )(sink, x, norm_g, w_in, w_out, cw, cb, lng, lnb, cos_t, sin_t, bias)


def _ab_sample_kernel(li, sink_ref, x_ref, g_ref, win_ref, wout_ref, cw_ref, cb_ref, lng_ref, lnb_ref, cos_ref, sin_ref,
                      ctx_ref, kc_ref, vc_ref, o_ref, cso_ref, ko_ref, vo_ref, q_scr, k_scr, v_scr, z_scr, y_scr, a_scr):
    sb = SB_SAMPLE
    heads_per_kv = N_HEADS // N_KV
    x = x_ref[...]
    h = _rms(x, g_ref[1:2, :]).astype(bf16)
    proj = _dot(h, win_ref[...].astype(bf16))

    cos_p = cos_ref[...]
    sin_p = sin_ref[...]
    lane = lax.broadcasted_iota(jnp.int32, (sb, LANES), 1)
    first = (lane % HEAD_DIM) < (ROPE_DIM // 2)
    lo = lane < HEAD_DIM
    for cblk in range(ATT_W // LANES):
        qc = _rope(proj[:, cblk * LANES:(cblk + 1) * LANES], cos_p, sin_p, first) * SCALE
        qc_sw = pltpu.roll(qc, HEAD_DIM, 1)
        for half in range(2):
            hd = 2 * cblk + half
            grp = hd // heads_per_kv
            src = qc if half == grp else qc_sw
            q_scr[:, hd * LANES:(hd + 1) * LANES] = jnp.where(lo if grp == 0 else jnp.logical_not(lo), src, 0.0)
    k_scr[...] = _rope(proj[:, ATT_W:ATT_W + KV_W], cos_p, sin_p, first)
    v_scr[...] = proj[:, ATT_W + KV_W:ATT_W + 2 * KV_W]
    o3 = ATT_W + 2 * KV_W
    z = proj[:, o3:o3 + C_CONV] * jax.nn.sigmoid(proj[:, o3 + C_CONV:o3 + 2 * C_CONV])
    z_scr[...] = z

    sub = lax.broadcasted_iota(jnp.int32, (N_HEADS, LANES), 0)
    sink_col = jnp.zeros((N_HEADS, 1), f32)
    sub1 = lax.broadcasted_iota(jnp.int32, (N_HEADS, 1), 0)
    for hd in range(N_HEADS):
        sink_col = jnp.where(sub1 == hd, sink_ref[li, hd], sink_col)
    cw_ctx = cw_ref[0:CONV_W - 1, :]
    cw_last = cw_ref[CONV_W - 1:CONV_W, :]

    def per_seq(b, carry):
        qrow = q_scr[pl.ds(b, 1), :]
        lhs = jnp.zeros((N_HEADS, LANES), f32)
        for hd in range(N_HEADS):
            lhs = jnp.where(sub == hd, qrow[:, hd * LANES:(hd + 1) * LANES], lhs)
        k_new = k_scr[pl.ds(b, 1), :]
        v_new = v_scr[pl.ds(b, 1), :]
        kc = kc_ref[b]
        vc = vc_ref[b]
        s = _dot_nt(lhs.astype(bf16), kc.astype(bf16))
        s_new = jnp.sum(lhs * k_new, axis=-1, keepdims=True)
        m = jnp.maximum(jnp.maximum(jnp.max(s, axis=-1, keepdims=True), s_new), sink_col)
        e = jnp.exp(s - m)
        e_new = jnp.exp(s_new - m)
        den = jnp.sum(e, axis=-1, keepdims=True) + e_new + jnp.exp(sink_col - m)
        o = _dot((e / den).astype(bf16), vc.astype(bf16)) + (e_new / den) * v_new
        a_scr[pl.ds(pl.multiple_of(b * N_HEADS, SUBLANES), N_HEADS), :] = o
        ko_ref[b, 0:WINDOW - 1, :] = kc_ref[b, 1:WINDOW, :]
        ko_ref[b, WINDOW - 1:WINDOW, :] = k_new
        vo_ref[b, 0:WINDOW - 1, :] = vc_ref[b, 1:WINDOW, :]
        vo_ref[b, WINDOW - 1:WINDOW, :] = v_new
        z_row = z_scr[pl.ds(b, 1), :]
        ctx = ctx_ref[b]
        y_row = jnp.sum(ctx * cw_ctx, axis=0, keepdims=True) + z_row * cw_last
        for ks in range(C_CONV // LANES):
            y_scr[pl.ds(pl.multiple_of(ks * sb * SUBLANES + b * SUBLANES, SUBLANES), SUBLANES), :] = (
                jnp.broadcast_to(y_row[:, ks * LANES:(ks + 1) * LANES], (SUBLANES, LANES)))
        cso_ref[b, 0:CONV_W - 2, :] = ctx_ref[b, 1:CONV_W - 1, :]
        cso_ref[b, CONV_W - 2:CONV_W - 1, :] = z_row
        return carry

    lax.fori_loop(0, sb, per_seq, 0)

    y = jnp.concatenate([y_scr[pl.ds(ks * sb * SUBLANES, sb, stride=SUBLANES), :] for ks in range(C_CONV // LANES)],
                        axis=1)
    blocks = []
    for cblk in range(ATT_W // LANES):
        parts = []
        for half in range(2):
            hd = 2 * cblk + half
            o_h = a_scr[pl.ds(hd, sb, stride=N_HEADS), :]
            parts.append(o_h if half == hd // heads_per_kv else pltpu.roll(o_h, HEAD_DIM, 1))
        blocks.append(jnp.where(lo, parts[0], parts[1]))
    a = jnp.concatenate(blocks, axis=1)
    c = _layernorm_silu(y + cb_ref[li:li + 1, :], lng_ref[li:li + 1, :], lnb_ref[li:li + 1, :])
    out = (_dot(a.astype(bf16), wout_ref[0:ATT_W, :].astype(bf16))
           + _dot(c.astype(bf16), wout_ref[ATT_W:, :].astype(bf16)))
    o_ref[...] = x + out


def _ab_sample(x, li, layer, sink, norm_g, w_in, w_out, cw, cb, lng, lnb, cos_s, sin_s, ctx, kc, vc):
    steps = DEC_BATCH // SB_SAMPLE
    first_blk = R_PROMPT // SB_SAMPLE
    seq3 = lambda i: (i, 0, 0)
    return pl.pallas_call(
        functools.partial(_ab_sample_kernel, li),
        out_shape=(
            jax.ShapeDtypeStruct((R_ALL, D_MODEL), f32),
            jax.ShapeDtypeStruct((DEC_BATCH, CONV_W - 1, C_CONV), f32),
            jax.ShapeDtypeStruct((DEC_BATCH, WINDOW, KV_W), f32),
            jax.ShapeDtypeStruct((DEC_BATCH, WINDOW, KV_W), f32),
        ),
        grid=(steps,),
        in_specs=[
            pl.BlockSpec(memory_space=pltpu.SMEM),
            pl.BlockSpec((SB_SAMPLE, D_MODEL), lambda i: (first_blk + i, 0)),
            _layer((3, D_MODEL), layer),
            _layer((D_MODEL, IN_AB), li),
            _layer((ATT_W + C_CONV, D_MODEL), li),
            _layer((CONV_W, C_CONV), li),
            _resident((N_EVEN, C_CONV)),
            _resident((N_EVEN, C_CONV)),
            _resident((N_EVEN, C_CONV)),
            _resident((1, LANES)),
            _resident((1, LANES)),
            pl.BlockSpec((None, SB_SAMPLE, CONV_W - 1, C_CONV), lambda i: (li, i, 0, 0)),
            pl.BlockSpec((SB_SAMPLE, WINDOW, KV_W), seq3),
            pl.BlockSpec((SB_SAMPLE, WINDOW, KV_W), seq3),
        ],
        out_specs=(
            pl.BlockSpec((SB_SAMPLE, D_MODEL), lambda i: (first_blk + i, 0)),
            pl.BlockSpec((SB_SAMPLE, CONV_W - 1, C_CONV), seq3),
            pl.BlockSpec((SB_SAMPLE, WINDOW, KV_W), seq3),
            pl.BlockSpec((SB_SAMPLE, WINDOW, KV_W), seq3),
        ),
        scratch_shapes=[
            pltpu.VMEM((SB_SAMPLE, N_HEADS * LANES), f32),
            pltpu.VMEM((SB_SAMPLE, KV_W), f32),
            pltpu.VMEM((SB_SAMPLE, KV_W), f32),
            pltpu.VMEM((SB_SAMPLE, C_CONV), f32),
            pltpu.VMEM((C_CONV // LANES * SB_SAMPLE * SUBLANES, LANES), f32),
            pltpu.VMEM((SB_SAMPLE * N_HEADS, LANES), f32),
        ],
        input_output_aliases={1: 0},
        compiler_params=_cparams(1),
        name="ab_sample",
    )(sink, x, norm_g, w_in, w_out, cw, cb, lng, lnb, cos_s, sin_s, ctx, kc, vc)


def _ssm_prep_kernel(are_ref, aim_ref, ldt_ref, bre_ref, bim_ref, lre_ref, lim_ref, pre_ref, pim_ref, bbre_ref, bbim_ref):
    a_re = are_ref[...]
    a_im = aim_ref[...]
    dt = jnp.exp(ldt_ref[...])
    mag = jnp.exp(a_re * dt)
    ang = a_im * dt
    lam_re = mag * jnp.cos(ang)
    lam_im = mag * jnp.sin(ang)
    den = a_re * a_re + a_im * a_im
    nr = lam_re - 1.0
    f_re = (nr * a_re + lam_im * a_im) / den
    f_im = (lam_im * a_re - nr * a_im) / den
    b_re = bre_ref[...]
    b_im = bim_ref[...]
    bbre_ref[...] = f_re * b_re - f_im * b_im
    bbim_ref[...] = f_re * b_im + f_im * b_re
    lre_ref[...] = lam_re
    lim_ref[...] = lam_im
    p_re, p_im = lam_re, lam_im
    r_re = r_im = None
    n = SEG_LEN
    while n:
        if n & 1:
            if r_re is None:
                r_re, r_im = p_re, p_im
            else:
                r_re, r_im = r_re * p_re - r_im * p_im, r_re * p_im + r_im * p_re
        n >>= 1
        if n:
            p_re, p_im = p_re * p_re - p_im * p_im, 2.0 * p_re * p_im
    pre_ref[...] = r_re
    pim_ref[...] = r_im


def _ssm_prep(a_re, a_im, log_dt, b_re, b_im, c_re, c_im):
    flat = lambda t: t.reshape(1, N_CPLX)
    ldt = jnp.repeat(log_dt, SSM_STATE).reshape(1, N_CPLX)
    b_t = lambda t: t.transpose(2, 0, 1).reshape(SSM_GROUP, N_CPLX)
    row = jax.ShapeDtypeStruct((1, N_CPLX), f32)
    mat = jax.ShapeDtypeStruct((SSM_GROUP, N_CPLX), f32)
    lre, lim, pre, pim, bbre, bbim = pl.pallas_call(
        _ssm_prep_kernel, out_shape=(row, row, row, row, mat, mat), name="ssm_prep",
    )(flat(a_re), flat(a_im), ldt, b_t(b_re), b_t(b_im))
    eye = jnp.eye(SSM_GROUP, dtype=f32)
    gq = N_GROUPS // N_QUART

    def in_mat(bb):
        t = bb.reshape(SSM_GROUP, N_QUART, gq, SSM_STATE).transpose(1, 2, 0, 3)
        return (t[:, :, :, None, :] * eye[None, :, None, :, None]).reshape(N_QUART, Q_CH, Q_ST).astype(bf16)

    def out_mat(cc):
        t = cc.reshape(N_QUART, gq, SSM_GROUP, SSM_STATE).transpose(0, 1, 3, 2)
        return (t[:, :, :, None, :] * eye[None, :, None, :, None]).reshape(N_QUART, Q_ST, Q_CH).astype(bf16)

    return lre, lim, pre, pim, in_mat(bbre), in_mat(bbim), out_mat(c_re), out_mat(c_im)


def _cmul_add(lr, li, hr, hi, xr, xi):
    return lr * hr - li * hi + xr, lr * hi + li * hr + xi


def _static_loop(n, body, carry):
    for t in range(n):
        carry = body(t, carry)
    return carry


def _ssm_prompt_kernel(lidx, x_ref, g_ref, win_ref, bbre_ref, bbim_ref, ccre_ref, ccim_ref, lre_ref, lim_ref, pre_ref,
                       pim_ref, d_ref, wglu_ref, o_ref, sre_ref, sim_ref, slab_ref, perm_ref, xr_ref, xi_ref, y_ref,
                       cr_ref, ci_ref):
    tt = TT_SSM
    i = pl.program_id(1)

    @pl.when(i == 0)
    def _():
        cr_ref[...] = jnp.zeros_like(cr_ref)
        ci_ref[...] = jnp.zeros_like(ci_ref)

    x = x_ref[...]
    h = _rms(x, g_ref[1:2, :]).astype(bf16)
    u = _dot(h, win_ref[...].astype(bf16))

    n_slab = D_MODEL // LANES
    for ks in range(n_slab):
        slab_ref[ks * tt:(ks + 1) * tt, :] = u[:, ks * LANES:(ks + 1) * LANES]

    def gather(t, carry):
        for ks in range(n_slab):
            perm_ref[pl.ds(t * SCAN_SEGS, SCAN_SEGS), ks * LANES:(ks + 1) * LANES] = (
                slab_ref[pl.ds(ks * tt + t, SCAN_SEGS, stride=SEG_LEN), :])
        return carry

    _static_loop(SEG_LEN, gather, 0)
    up = perm_ref[...]
    ub = up.astype(bf16)

    for qd in range(N_QUART):
        st = slice(qd * Q_ST, (qd + 1) * Q_ST)
        ubq = ub[:, qd * Q_CH:(qd + 1) * Q_CH]
        xr_ref[...] = _dot(ubq, bbre_ref[qd])
        xi_ref[...] = _dot(ubq, bbim_ref[qd])
        lr = jnp.broadcast_to(lre_ref[:, st], (SCAN_SEGS, Q_ST))
        li = jnp.broadcast_to(lim_ref[:, st], (SCAN_SEGS, Q_ST))

        def local_end(t, hc):
            rows = pl.ds(t * SCAN_SEGS, SCAN_SEGS)
            return _cmul_add(lr, li, hc[0], hc[1], xr_ref[rows, :], xi_ref[rows, :])

        zero = jnp.zeros((SCAN_SEGS, Q_ST), f32)
        er, ei = _static_loop(SEG_LEN, local_end, (zero, zero))

        pr = pre_ref[:, st]
        pi = pim_ref[:, st]
        c_r = cr_ref[:, st]
        c_i = ci_ref[:, st]
        rows_r, rows_i = [c_r], [c_i]
        for j in range(1, SCAN_SEGS):
            c_r, c_i = _cmul_add(pr, pi, c_r, c_i, er[j - 1:j], ei[j - 1:j])
            rows_r.append(c_r)
            rows_i.append(c_i)
        c_r, c_i = _cmul_add(pr, pi, c_r, c_i, er[SCAN_SEGS - 1:], ei[SCAN_SEGS - 1:])
        cr_ref[:, st] = c_r
        ci_ref[:, st] = c_i
        start = (jnp.concatenate(rows_r, axis=0), jnp.concatenate(rows_i, axis=0))

        def scan(t, hc):
            rows = pl.ds(t * SCAN_SEGS, SCAN_SEGS)
            hr, hi = _cmul_add(lr, li, hc[0], hc[1], xr_ref[rows, :], xi_ref[rows, :])
            xr_ref[rows, :] = hr
            xi_ref[rows, :] = hi
            return hr, hi

        _static_loop(SEG_LEN, scan, start)
        y_ref[:, qd * Q_CH:(qd + 1) * Q_CH] = (_dot(xr_ref[...].astype(bf16), ccre_ref[qd])
                                               - _dot(xi_ref[...].astype(bf16), ccim_ref[qd]))

    y = y_ref[...] + d_ref[lidx:lidx + 1, :] * up
    vg = _dot(jax.nn.gelu(y).astype(bf16), wglu_ref[...].astype(bf16))
    perm_ref[...] = vg[:, :D_MODEL] * jax.nn.sigmoid(vg[:, D_MODEL:])

    def scatter(t, carry):
        for ks in range(n_slab):
            slab_ref[pl.ds(ks * tt + t, SCAN_SEGS, stride=SEG_LEN), :] = (
                perm_ref[pl.ds(t * SCAN_SEGS, SCAN_SEGS), ks * LANES:(ks + 1) * LANES])
        return carry

    _static_loop(SEG_LEN, scatter, 0)
    out = jnp.concatenate([slab_ref[ks * tt:(ks + 1) * tt, :] for ks in range(n_slab)], axis=1)
    row = i * tt + lax.broadcasted_iota(jnp.int32, (tt, 1), 0)
    o_ref[...] = jnp.where(row >= PAD_ROWS, x + out, 0.0)

    @pl.when(i == pl.num_programs(1) - 1)
    def _():
        sre_ref[0] = cr_ref[...]
        sim_ref[0] = ci_ref[...]


def _ssm_prompt(x, li, layer, norm_g, w_in, bbre, bbim, ccre, ccim, lre, lim, pre, pim, d, w_glu):
    steps = ROWS_B // TT_SSM
    state = jax.ShapeDtypeStruct((BATCH, 1, N_CPLX), f32)
    return pl.pallas_call(
        functools.partial(_ssm_prompt_kernel, li),
        out_shape=(jax.ShapeDtypeStruct((R_ALL, D_MODEL), f32), state, state),
        grid=(BATCH, steps),
        in_specs=[
            pl.BlockSpec((TT_SSM, D_MODEL), lambda b, i: (b * steps + i, 0)),
            _layer((3, D_MODEL), layer),
            _layer((D_MODEL, D_MODEL), li),
            _resident((N_QUART, Q_CH, Q_ST)),
            _resident((N_QUART, Q_CH, Q_ST)),
            _resident((N_QUART, Q_ST, Q_CH)),
            _resident((N_QUART, Q_ST, Q_CH)),
            _resident((1, N_CPLX)),
            _resident((1, N_CPLX)),
            _resident((1, N_CPLX)),
            _resident((1, N_CPLX)),
            _resident((N_ODD, D_MODEL)),
            _layer((D_MODEL, 2 * D_MODEL), li),
        ],
        out_specs=(
            pl.BlockSpec((TT_SSM, D_MODEL), lambda b, i: (b * steps + i, 0)),
            pl.BlockSpec((1, 1, N_CPLX), lambda b, i: (b, 0, 0)),
            pl.BlockSpec((1, 1, N_CPLX), lambda b, i: (b, 0, 0)),
        ),
        scratch_shapes=[
            pltpu.VMEM((D_MODEL // LANES * TT_SSM, LANES), f32),
            pltpu.VMEM((TT_SSM, D_MODEL), f32),
            pltpu.VMEM((TT_SSM, Q_ST), f32),
            pltpu.VMEM((TT_SSM, Q_ST), f32),
            pltpu.VMEM((TT_SSM, D_MODEL), f32),
            pltpu.VMEM((1, N_CPLX), f32),
            pltpu.VMEM((1, N_CPLX), f32),
        ],
        input_output_aliases={0: 0},
        compiler_params=_cparams(2),
        name="ssm_prompt",
    )(x, norm_g, w_in, bbre, bbim, ccre, ccim, lre, lim, pre, pim, d, w_glu)


def _ssm_sample_kernel(li, x_ref, g_ref, win_ref, bbre_ref, bbim_ref, ccre_ref, ccim_ref, lre_ref, lim_ref, d_ref,
                       wglu_ref, sre_in, sim_in, o_ref, sre_ref, sim_ref, y_ref):
    x = x_ref[...]
    h = _rms(x, g_ref[1:2, :]).astype(bf16)
    u = _dot(h, win_ref[...].astype(bf16))
    ub = u.astype(bf16)
    for qd in range(N_QUART):
        st = slice(qd * Q_ST, (qd + 1) * Q_ST)
        ubq = ub[:, qd * Q_CH:(qd + 1) * Q_CH]
        hr, hi = _cmul_add(lre_ref[:, st], lim_ref[:, st], sre_in[:, st], sim_in[:, st],
                           _dot(ubq, bbre_ref[qd]), _dot(ubq, bbim_ref[qd]))
        sre_ref[:, st] = hr
        sim_ref[:, st] = hi
        y_ref[:, qd * Q_CH:(qd + 1) * Q_CH] = _dot(hr.astype(bf16), ccre_ref[qd]) - _dot(hi.astype(bf16), ccim_ref[qd])
    y = y_ref[...] + d_ref[li:li + 1, :] * u
    vg = _dot(jax.nn.gelu(y).astype(bf16), wglu_ref[...].astype(bf16))
    o_ref[...] = x + vg[:, :D_MODEL] * jax.nn.sigmoid(vg[:, D_MODEL:])


def _ssm_sample(x, li, layer, norm_g, w_in, bbre, bbim, ccre, ccim, lre, lim, d, w_glu, s_re, s_im):
    state = jax.ShapeDtypeStruct((DEC_BATCH, N_CPLX), f32)
    blk = R_PROMPT // DEC_BATCH
    return pl.pallas_call(
        functools.partial(_ssm_sample_kernel, li),
        out_shape=(jax.ShapeDtypeStruct((R_ALL, D_MODEL), f32), state, state),
        grid=(1,),
        in_specs=[
            pl.BlockSpec((DEC_BATCH, D_MODEL), lambda i: (blk, 0)),
            _layer((3, D_MODEL), layer),
            _layer((D_MODEL, D_MODEL), li),
            _resident((N_QUART, Q_CH, Q_ST)),
            _resident((N_QUART, Q_CH, Q_ST)),
            _resident((N_QUART, Q_ST, Q_CH)),
            _resident((N_QUART, Q_ST, Q_CH)),
            _resident((1, N_CPLX)),
            _resident((1, N_CPLX)),
            _resident((N_ODD, D_MODEL)),
            _layer((D_MODEL, 2 * D_MODEL), li),
            _layer((DEC_BATCH, N_CPLX), li),
            _layer((DEC_BATCH, N_CPLX), li),
        ],
        out_specs=(
            pl.BlockSpec((DEC_BATCH, D_MODEL), lambda i: (blk, 0)),
            pl.BlockSpec((DEC_BATCH, N_CPLX), lambda i: (0, 0)),
            pl.BlockSpec((DEC_BATCH, N_CPLX), lambda i: (0, 0)),
        ),
        scratch_shapes=[pltpu.VMEM((DEC_BATCH, D_MODEL), f32)],
        input_output_aliases={0: 0},
        compiler_params=_cparams(1),
        name="ssm_sample",
    )(x, norm_g, w_in, bbre, bbim, ccre, ccim, lre, lim, d, w_glu, s_re, s_im)


def _rope_tables():
    half = ROPE_DIM // 2
    inv = ROPE_THETA ** (-jnp.arange(half, dtype=f32) * 2.0 / ROPE_DIM)
    pos = jnp.concatenate([jnp.arange(ROWS_B, dtype=jnp.int32) - PAD_ROWS, jnp.full((1,), PAST_LEN, jnp.int32)])
    ang = pos.astype(f32)[:, None] * inv[None, :]
    cos, sin = jnp.cos(ang), jnp.sin(ang)
    ones = jnp.ones((pos.shape[0], HEAD_DIM - ROPE_DIM), f32)
    cos_h = jnp.concatenate([cos, cos, ones], axis=1)
    sin_h = jnp.concatenate([-sin, sin, 0.0 * ones], axis=1)
    cos_t = jnp.concatenate([cos_h, cos_h], axis=1)
    sin_t = jnp.concatenate([sin_h, sin_h], axis=1)
    return cos_t[:ROWS_B], sin_t[:ROWS_B], cos_t[ROWS_B:], sin_t[ROWS_B:]


def _window_bias():
    r = jnp.arange(BLOCK, dtype=jnp.int32)[:, None]
    s = jnp.arange(2 * BLOCK, dtype=jnp.int32)[None, :]
    return jnp.where((s >= r) & (s <= r + WINDOW), 0.0, NEG_BIG).astype(f32)


def kernel(x_prompt, x_sample, state_conv, cache_win_k, cache_win_v, state_ssm_re, state_ssm_im, meta_tokens, norm_g, final_norm_g, ffn1_w_gu, ffn1_w_down, ffn2_w_gu, ffn2_w_down, ab_w_in, ab_w_out, attn_sink, conv_w, conv_b, conv_ln_g, conv_ln_b, ssm_w_in, ssm_a_re, ssm_a_im, ssm_log_dt, ssm_b_re, ssm_b_im, ssm_c_re, ssm_c_im, ssm_d, ssm_w_glu):
    meta = jnp.broadcast_to(meta_tokens[None], (BATCH, N_META, D_MODEL))
    pad = jnp.zeros((BATCH, PAD_ROWS, D_MODEL), f32)
    x = jnp.concatenate([jnp.concatenate([pad, meta, x_prompt], axis=1).reshape(R_PROMPT, D_MODEL),
                         x_sample.reshape(DEC_BATCH, D_MODEL)], axis=0)

    cos_t, sin_t, cos_s, sin_s = _rope_tables()
    bias = _window_bias()
    row = lambda t: t.reshape(1, -1)

    p_conv, p_k, p_v, p_re, p_im = [], [], [], [], []
    s_conv, s_k, s_v, s_re, s_im = [], [], [], [], []
    s_re_in = state_ssm_re.reshape(N_ODD, DEC_BATCH, N_CPLX)
    s_im_in = state_ssm_im.reshape(N_ODD, DEC_BATCH, N_CPLX)
    for l in range(DEPTH):
        x = _ffn(x, norm_g, ffn1_w_gu, ffn1_w_down, l, 0)
        i = l // 2
        if l % 2 == 0:
            wargs = (attn_sink, norm_g, ab_w_in, ab_w_out, conv_w, conv_b, conv_ln_g, conv_ln_b)
            x, cst, kst, vst = _ab_prompt(x, i, l, *wargs, cos_t, sin_t, bias)
            x, cso, kso, vso = _ab_sample(x, i, l, *wargs, cos_s, sin_s, state_conv,
                                          cache_win_k[i].reshape(DEC_BATCH, WINDOW, KV_W),
                                          cache_win_v[i].reshape(DEC_BATCH, WINDOW, KV_W))
            p_conv.append(cst)
            p_k.append(kst.reshape(BATCH, WINDOW, N_KV, HEAD_DIM))
            p_v.append(vst.reshape(BATCH, WINDOW, N_KV, HEAD_DIM))
            s_conv.append(cso)
            s_k.append(kso.reshape(DEC_BATCH, WINDOW, N_KV, HEAD_DIM))
            s_v.append(vso.reshape(DEC_BATCH, WINDOW, N_KV, HEAD_DIM))
        else:
            lre, lim, pre, pim, bbre, bbim, ccre, ccim = _ssm_prep(
                ssm_a_re[i], ssm_a_im[i], ssm_log_dt[i], ssm_b_re[i], ssm_b_im[i], ssm_c_re[i], ssm_c_im[i])
            x, pre_s, pim_s = _ssm_prompt(x, i, l, norm_g, ssm_w_in, bbre, bbim, ccre, ccim, lre, lim, pre, pim,
                                          ssm_d, ssm_w_glu)
            x, sre_s, sim_s = _ssm_sample(x, i, l, norm_g, ssm_w_in, bbre, bbim, ccre, ccim, lre, lim,
                                          ssm_d, ssm_w_glu, s_re_in, s_im_in)
            p_re.append(pre_s.reshape(BATCH, N_GROUPS, SSM_STATE))
            p_im.append(pim_s.reshape(BATCH, N_GROUPS, SSM_STATE))
            s_re.append(sre_s.reshape(DEC_BATCH, N_GROUPS, SSM_STATE))
            s_im.append(sim_s.reshape(DEC_BATCH, N_GROUPS, SSM_STATE))
        x = _ffn(x, norm_g, ffn2_w_gu, ffn2_w_down, l, 2)

    y_prompt, y_sample = _final_norm(x, row(final_norm_g))
    return (y_prompt, y_sample,
            jnp.stack(p_conv), jnp.stack(p_k), jnp.stack(p_v), jnp.stack(p_re), jnp.stack(p_im),
            jnp.stack(s_conv), jnp.stack(s_k), jnp.stack(s_v), jnp.stack(s_re), jnp.stack(s_im))
```

```python
import functools
import math

import jax
import jax.numpy as jnp
from jax import lax
from jax.experimental import pallas as pl
from jax.experimental.pallas import tpu as pltpu

f32 = jnp.float32
bf16 = jnp.bfloat16

D_MODEL = 1024
BATCH = 4
SEQ = 4096
DEPTH = 4
DEC_BATCH = 128
PAST_LEN = 8192
N_META = 16
EPS = 1e-6
D_FF = 2816
N_HEADS = 8
N_KV = 2
HEAD_DIM = 64
WINDOW = 128
BLOCK = 128
ROPE_DIM = HEAD_DIM // 4
ROPE_THETA = 500000.0
SCALE = HEAD_DIM ** -0.5
C_CONV = 512
CONV_W = 31
SSM_GROUP = 16
N_GROUPS = D_MODEL // SSM_GROUP
SSM_STATE = 64
ATT_W = N_HEADS * HEAD_DIM
KV_W = N_KV * HEAD_DIM
IN_AB = ATT_W + 2 * KV_W + 2 * C_CONV
N_CPLX = N_GROUPS * SSM_STATE
N_EVEN = (DEPTH + 1) // 2
N_ODD = DEPTH // 2

LANES = 128
SUBLANES = 8
VMEM_LIMIT = 56 * 1024 * 1024

L_PROMPT = N_META + SEQ
PAD_ROWS = (-L_PROMPT) % BLOCK
ROWS_B = PAD_ROWS + L_PROMPT
R_PROMPT = BATCH * ROWS_B
R_ALL = R_PROMPT + DEC_BATCH

TM_FFN = 448
TT_AB = 384
CONV_HIST = 32
CONV_ROWS = 64
SCAN_SEGS = SUBLANES
SEG_LEN = 66
TT_SSM = SCAN_SEGS * SEG_LEN
N_QUART = 4
Q_CH = D_MODEL // N_QUART
Q_ST = N_CPLX // N_QUART
NORM_BLOCKS = 8
SB_SAMPLE = 32
NEG_BIG = -1e30


def _cparams(n_axes):
    return pltpu.CompilerParams(dimension_semantics=("arbitrary",) * n_axes, vmem_limit_bytes=VMEM_LIMIT)


def _resident(shape):
    zeros = (0,) * len(shape)
    return pl.BlockSpec(shape, lambda *_: zeros, pipeline_mode=pl.Buffered(1))


def _layer(shape, layer):
    idx = (layer,) + (0,) * len(shape)
    return pl.BlockSpec((None,) + tuple(shape), lambda *_: idx, pipeline_mode=pl.Buffered(1))


def _rms(x, g):
    ms = jnp.mean(x * x, axis=-1, keepdims=True)
    return x * lax.rsqrt(ms + EPS) * g


def _dot(a, b):
    return jnp.dot(a, b, preferred_element_type=f32)


def _dot_nt(a, b):
    return lax.dot_general(a, b, (((1,), (1,)), ((), ())), preferred_element_type=f32)


def _ffn_kernel(norm_row, x_ref, g_ref, wgu_ref, wd_ref, o_ref):
    x = x_ref[...]
    h = _rms(x, g_ref[norm_row:norm_row + 1, :]).astype(bf16)
    gate = _dot(h, wgu_ref[:, :D_FF].astype(bf16))
    up = _dot(h, wgu_ref[:, D_FF:].astype(bf16))
    act = (gate * jax.nn.sigmoid(gate) * up).astype(bf16)
    o_ref[...] = x + 0.5 * _dot(act, wd_ref[...].astype(bf16))


def _ffn(x, norm_g, w_gu, w_down, layer, norm_row):
    return pl.pallas_call(
        functools.partial(_ffn_kernel, norm_row),
        out_shape=jax.ShapeDtypeStruct((R_ALL, D_MODEL), f32),
        grid=(R_ALL // TM_FFN,),
        in_specs=[
            pl.BlockSpec((TM_FFN, D_MODEL), lambda i: (i, 0)),
            _layer((3, D_MODEL), layer),
            _layer((D_MODEL, 2 * D_FF), layer),
            _layer((D_FF, D_MODEL), layer),
        ],
        out_specs=pl.BlockSpec((TM_FFN, D_MODEL), lambda i: (i, 0)),
        input_output_aliases={0: 0},
        compiler_params=_cparams(1),
        name="ffn",
    )(x, norm_g, w_gu, w_down)


def _norm_kernel(x_ref, g_ref, o_ref):
    o_ref[...] = _rms(x_ref[...], g_ref[...]).reshape(o_ref.shape)


def _final_norm(x, g):
    blocks_b = ROWS_B // BLOCK
    nb = NORM_BLOCKS
    steps = SEQ // (nb * BLOCK)
    y_prompt = pl.pallas_call(
        _norm_kernel,
        out_shape=jax.ShapeDtypeStruct((BATCH * SEQ // BLOCK, BLOCK, D_MODEL), f32),
        grid=(BATCH, steps),
        in_specs=[pl.BlockSpec((pl.Element(nb), pl.Element(BLOCK), pl.Element(D_MODEL)),
                               lambda b, j: (b * blocks_b + 1 + nb * j, 0, 0)),
                  _resident((1, D_MODEL))],
        out_specs=pl.BlockSpec((nb, BLOCK, D_MODEL), lambda b, j: (b * steps + j, 0, 0)),
        compiler_params=_cparams(2),
        name="final_norm_prompt",
    )(x.reshape(R_ALL // BLOCK, BLOCK, D_MODEL), g).reshape(BATCH, SEQ, D_MODEL)
    y_sample = pl.pallas_call(
        _norm_kernel,
        out_shape=jax.ShapeDtypeStruct((DEC_BATCH, D_MODEL), f32),
        grid=(1,),
        in_specs=[pl.BlockSpec((DEC_BATCH, D_MODEL), lambda i: (R_PROMPT // DEC_BATCH, 0)), _resident((1, D_MODEL))],
        out_specs=pl.BlockSpec((DEC_BATCH, D_MODEL), lambda i: (0, 0)),
        compiler_params=_cparams(1),
        name="final_norm_sample",
    )(x, g)
    return y_prompt, y_sample.reshape(DEC_BATCH, 1, D_MODEL)


def _rope(t, cos_p, sin_p, first):
    half = ROPE_DIM // 2
    partner = jnp.where(first, pltpu.roll(t, LANES - half, 1), pltpu.roll(t, half, 1))
    return t * cos_p + partner * sin_p


def _layernorm_silu(y, g, b):
    mu = jnp.mean(y, axis=-1, keepdims=True)
    yc = y - mu
    yn = yc * lax.rsqrt(jnp.mean(yc * yc, axis=-1, keepdims=True) + EPS) * g + b
    return yn * jax.nn.sigmoid(yn)


def _softmax_sink(s, sink):
    m = jnp.maximum(jnp.max(s, axis=-1, keepdims=True), sink)
    e = jnp.exp(s - m)
    den = jnp.sum(e, axis=-1, keepdims=True) + jnp.exp(sink - m)
    return e / den


def _conv_taps():
    taps = []
    for w in range(CONV_W):
        idx = CONV_HIST - (CONV_W - 1) + w
        taps.append((w, idx % SUBLANES, idx - idx % SUBLANES))
    return taps


def _ab_prompt_kernel(li, sink_ref, x_ref, g_ref, win_ref, wout_ref, cw_ref, cb_ref, lng_ref, lnb_ref, cos_ref, sin_ref,
                      bias_ref, o_ref, cst_ref, kst_ref, vst_ref, kk_ref, vv_ref, z_ref, zs_ref, y_ref, a_ref):
    tt = TT_AB
    i = pl.program_id(1)

    @pl.when(i == 0)
    def _():
        kk_ref[:, 0:BLOCK, :] = jnp.zeros((N_KV, BLOCK, LANES), bf16)
        vv_ref[:, 0:BLOCK, :] = jnp.zeros((N_KV, BLOCK, LANES), bf16)
        z_ref[0:CONV_HIST, :] = jnp.zeros((CONV_HIST, C_CONV), f32)

    x = x_ref[...]
    h = _rms(x, g_ref[1:2, :]).astype(bf16)
    proj = _dot(h, win_ref[...].astype(bf16))

    cos_p = cos_ref[...]
    sin_p = sin_ref[...]
    lane = lax.broadcasted_iota(jnp.int32, (tt, LANES), 1)
    first = (lane % HEAD_DIM) < (ROPE_DIM // 2)
    lo = lane < HEAD_DIM

    q = [_rope(proj[:, c * LANES:(c + 1) * LANES], cos_p, sin_p, first) * SCALE for c in range(ATT_W // LANES)]
    k = _rope(proj[:, ATT_W:ATT_W + KV_W], cos_p, sin_p, first)
    v = proj[:, ATT_W + KV_W:ATT_W + 2 * KV_W]
    k_sw = pltpu.roll(k, HEAD_DIM, 1)
    v_sw = pltpu.roll(v, HEAD_DIM, 1)
    kk_ref[0, BLOCK:, :] = jnp.where(lo, k, k_sw).astype(bf16)
    kk_ref[1, BLOCK:, :] = jnp.where(lo, k_sw, k).astype(bf16)
    vv_ref[0, BLOCK:, :] = jnp.where(lo, v, v_sw).astype(bf16)
    vv_ref[1, BLOCK:, :] = jnp.where(lo, v_sw, v).astype(bf16)

    o3 = ATT_W + 2 * KV_W
    z = proj[:, o3:o3 + C_CONV] * jax.nn.sigmoid(proj[:, o3 + C_CONV:o3 + 2 * C_CONV])
    z_ref[CONV_HIST:, :] = z

    bias = bias_ref[...]
    col = lax.broadcasted_iota(jnp.int32, (BLOCK, 2 * BLOCK), 1)
    lo_b = lax.broadcasted_iota(jnp.int32, (BLOCK, LANES), 1) < HEAD_DIM
    for jb in range(tt // BLOCK):
        kmin = PAD_ROWS + BLOCK - (i * tt + jb * BLOCK)
        key_ok = col >= kmin
        for grp in range(N_KV):
            k2 = kk_ref[grp, jb * BLOCK:(jb + 2) * BLOCK, :]
            v2 = vv_ref[grp, jb * BLOCK:(jb + 2) * BLOCK, :]
            for cpart in range(2):
                cblk = grp * 2 + cpart
                qc = q[cblk][jb * BLOCK:(jb + 1) * BLOCK]
                outs = []
                for half in range(2):
                    qm = jnp.where(lo_b if half == 0 else jnp.logical_not(lo_b), qc, 0.0).astype(bf16)
                    s = _dot_nt(qm, k2)
                    s = jnp.where(key_ok, s + bias, NEG_BIG)
                    p = _softmax_sink(s, sink_ref[li, cblk * 2 + half])
                    outs.append(_dot(p.astype(bf16), v2))
                a_ref[jb * BLOCK:(jb + 1) * BLOCK, cblk * LANES:(cblk + 1) * LANES] = (
                    jnp.where(lo_b, outs[0], outs[1]).astype(bf16))

    n_sh = CONV_HIST + tt - SUBLANES
    for b in range(1, SUBLANES):
        zs_ref[b, 0:n_sh, :] = z_ref[pl.ds(b, n_sh), :]
    cb = cb_ref[li:li + 1, :]
    for c0 in range(0, tt, CONV_ROWS):
        acc = jnp.broadcast_to(cb, (CONV_ROWS, C_CONV))
        for w, b, a8 in _conv_taps():
            rows = slice(c0 + a8, c0 + a8 + CONV_ROWS)
            src = z_ref[rows, :] if b == 0 else zs_ref[b, rows, :]
            acc = acc + src * cw_ref[w:w + 1, :]
        y_ref[c0:c0 + CONV_ROWS, :] = acc
    c = _layernorm_silu(y_ref[...], lng_ref[li:li + 1, :], lnb_ref[li:li + 1, :])

    out = (_dot(a_ref[...], wout_ref[0:ATT_W, :].astype(bf16))
           + _dot(c.astype(bf16), wout_ref[ATT_W:, :].astype(bf16)))
    row = i * tt + lax.broadcasted_iota(jnp.int32, (tt, 1), 0)
    o_ref[...] = jnp.where(row >= PAD_ROWS, x + out, 0.0)

    z_ref[0:CONV_HIST, :] = z_ref[tt:tt + CONV_HIST, :]
    kk_ref[:, 0:BLOCK, :] = kk_ref[:, tt:tt + BLOCK, :]
    vv_ref[:, 0:BLOCK, :] = vv_ref[:, tt:tt + BLOCK, :]

    @pl.when(i == pl.num_programs(1) - 1)
    def _():
        kst_ref[0] = k[tt - BLOCK:]
        vst_ref[0] = v[tt - BLOCK:]
        cst_ref[0] = z_ref[pl.ds(tt + CONV_HIST - (CONV_W - 1), CONV_W - 1), :]


def _ab_prompt(x, li, layer, sink, norm_g, w_in, w_out, cw, cb, lng, lnb, cos_t, sin_t, bias):
    steps = ROWS_B // TT_AB
    return pl.pallas_call(
        functools.partial(_ab_prompt_kernel, li),
        out_shape=(
            jax.ShapeDtypeStruct((R_ALL, D_MODEL), f32),
            jax.ShapeDtypeStruct((BATCH, CONV_W - 1, C_CONV), f32),
            jax.ShapeDtypeStruct((BATCH, BLOCK, KV_W), f32),
            jax.ShapeDtypeStruct((BATCH, BLOCK, KV_W), f32),
        ),
        grid=(BATCH, steps),
        in_specs=[
            pl.BlockSpec(memory_space=pltpu.SMEM),
            pl.BlockSpec((TT_AB, D_MODEL), lambda b, i: (b * steps + i, 0)),
            _layer((3, D_MODEL), layer),
            _layer((D_MODEL, IN_AB), li),
            _layer((ATT_W + C_CONV, D_MODEL), li),
            _layer((CONV_W, C_CONV), li),
            _resident((N_EVEN, C_CONV)),
            _resident((N_EVEN, C_CONV)),
            _resident((N_EVEN, C_CONV)),
            pl.BlockSpec((TT_AB, LANES), lambda b, i: (i, 0)),
            pl.BlockSpec((TT_AB, LANES), lambda b, i: (i, 0)),
            _resident((BLOCK, 2 * BLOCK)),
        ],
        out_specs=(
            pl.BlockSpec((TT_AB, D_MODEL), lambda b, i: (b * steps + i, 0)),
            pl.BlockSpec((1, CONV_W - 1, C_CONV), lambda b, i: (b, 0, 0)),
            pl.BlockSpec((1, BLOCK, KV_W), lambda b, i: (b, 0, 0)),
            pl.BlockSpec((1, BLOCK, KV_W), lambda b, i: (b, 0, 0)),
        ),
        scratch_shapes=[
            pltpu.VMEM((N_KV, BLOCK + TT_AB, LANES), bf16),
            pltpu.VMEM((N_KV, BLOCK + TT_AB, LANES), bf16),
            pltpu.VMEM((CONV_HIST + TT_AB, C_CONV), f32),
            pltpu.VMEM((SUBLANES, CONV_HIST + TT_AB, C_CONV), f32),
            pltpu.VMEM((TT_AB, C_CONV), f32),
            pltpu.VMEM((TT_AB, ATT_W), bf16),
        ],
        input_output_aliases={1: 0},
        compiler_params=_cparams(2),
        name="ab_prompt",
    )(sink, x, norm_g, w_in, w_out, cw, cb, lng, lnb, cos_t, sin_t, bias)


def _ab_sample_kernel(li, sink_ref, x_ref, g_ref, win_ref, wout_ref, cw_ref, cb_ref, lng_ref, lnb_ref, cos_ref, sin_ref,
                      ctx_ref, kc_ref, vc_ref, o_ref, cso_ref, ko_ref, vo_ref, q_scr, k_scr, v_scr, z_scr, y_scr, a_scr):
    sb = SB_SAMPLE
    heads_per_kv = N_HEADS // N_KV
    x = x_ref[...]
    h = _rms(x, g_ref[1:2, :]).astype(bf16)
    proj = _dot(h, win_ref[...].astype(bf16))

    cos_p = cos_ref[...]
    sin_p = sin_ref[...]
    lane = lax.broadcasted_iota(jnp.int32, (sb, LANES), 1)
    first = (lane % HEAD_DIM) < (ROPE_DIM // 2)
    lo = lane < HEAD_DIM
    for cblk in range(ATT_W // LANES):
        qc = _rope(proj[:, cblk * LANES:(cblk + 1) * LANES], cos_p, sin_p, first) * SCALE
        qc_sw = pltpu.roll(qc, HEAD_DIM, 1)
        for half in range(2):
            hd = 2 * cblk + half
            grp = hd // heads_per_kv
            src = qc if half == grp else qc_sw
            q_scr[:, hd * LANES:(hd + 1) * LANES] = jnp.where(lo if grp == 0 else jnp.logical_not(lo), src, 0.0)
    k_scr[...] = _rope(proj[:, ATT_W:ATT_W + KV_W], cos_p, sin_p, first)
    v_scr[...] = proj[:, ATT_W + KV_W:ATT_W + 2 * KV_W]
    o3 = ATT_W + 2 * KV_W
    z = proj[:, o3:o3 + C_CONV] * jax.nn.sigmoid(proj[:, o3 + C_CONV:o3 + 2 * C_CONV])
    z_scr[...] = z

    sub = lax.broadcasted_iota(jnp.int32, (N_HEADS, LANES), 0)
    sink_col = jnp.zeros((N_HEADS, 1), f32)
    sub1 = lax.broadcasted_iota(jnp.int32, (N_HEADS, 1), 0)
    for hd in range(N_HEADS):
        sink_col = jnp.where(sub1 == hd, sink_ref[li, hd], sink_col)
    cw_ctx = cw_ref[0:CONV_W - 1, :]
    cw_last = cw_ref[CONV_W - 1:CONV_W, :]

    def per_seq(b, carry):
        qrow = q_scr[pl.ds(b, 1), :]
        lhs = jnp.zeros((N_HEADS, LANES), f32)
        for hd in range(N_HEADS):
            lhs = jnp.where(sub == hd, qrow[:, hd * LANES:(hd + 1) * LANES], lhs)
        k_new = k_scr[pl.ds(b, 1), :]
        v_new = v_scr[pl.ds(b, 1), :]
        kc = kc_ref[b]
        vc = vc_ref[b]
        s = _dot_nt(lhs.astype(bf16), kc.astype(bf16))
        s_new = jnp.sum(lhs * k_new, axis=-1, keepdims=True)
        m = jnp.maximum(jnp.maximum(jnp.max(s, axis=-1, keepdims=True), s_new), sink_col)
        e = jnp.exp(s - m)
        e_new = jnp.exp(s_new - m)
        den = jnp.sum(e, axis=-1, keepdims=True) + e_new + jnp.exp(sink_col - m)
        o = _dot((e / den).astype(bf16), vc.astype(bf16)) + (e_new / den) * v_new
        a_scr[pl.ds(pl.multiple_of(b * N_HEADS, SUBLANES), N_HEADS), :] = o
        ko_ref[b, 0:WINDOW - 1, :] = kc_ref[b, 1:WINDOW, :]
        ko_ref[b, WINDOW - 1:WINDOW, :] = k_new
        vo_ref[b, 0:WINDOW - 1, :] = vc_ref[b, 1:WINDOW, :]
        vo_ref[b, WINDOW - 1:WINDOW, :] = v_new
        z_row = z_scr[pl.ds(b, 1), :]
        ctx = ctx_ref[b]
        y_row = jnp.sum(ctx * cw_ctx, axis=0, keepdims=True) + z_row * cw_last
        for ks in range(C_CONV // LANES):
            y_scr[pl.ds(pl.multiple_of(ks * sb * SUBLANES + b * SUBLANES, SUBLANES), SUBLANES), :] = (
                jnp.broadcast_to(y_row[:, ks * LANES:(ks + 1) * LANES], (SUBLANES, LANES)))
        cso_ref[b, 0:CONV_W - 2, :] = ctx_ref[b, 1:CONV_W - 1, :]
        cso_ref[b, CONV_W - 2:CONV_W - 1, :] = z_row
        return carry

    lax.fori_loop(0, sb, per_seq, 0)

    y = jnp.concatenate([y_scr[pl.ds(ks * sb * SUBLANES, sb, stride=SUBLANES), :] for ks in range(C_CONV // LANES)],
                        axis=1)
    blocks = []
    for cblk in range(ATT_W // LANES):
        parts = []
        for half in range(2):
            hd = 2 * cblk + half
            o_h = a_scr[pl.ds(hd, sb, stride=N_HEADS), :]
            parts.append(o_h if half == hd // heads_per_kv else pltpu.roll(o_h, HEAD_DIM, 1))
        blocks.append(jnp.where(lo, parts[0], parts[1]))
    a = jnp.concatenate(blocks, axis=1)
    c = _layernorm_silu(y + cb_ref[li:li + 1, :], lng_ref[li:li + 1, :], lnb_ref[li:li + 1, :])
    out = (_dot(a.astype(bf16), wout_ref[0:ATT_W, :].astype(bf16))
           + _dot(c.astype(bf16), wout_ref[ATT_W:, :].astype(bf16)))
    o_ref[...] = x + out


def _ab_sample(x, li, layer, sink, norm_g, w_in, w_out, cw, cb, lng, lnb, cos_s, sin_s, ctx, kc, vc):
    steps = DEC_BATCH // SB_SAMPLE
    first_blk = R_PROMPT // SB_SAMPLE
    seq3 = lambda i: (i, 0, 0)
    return pl.pallas_call(
        functools.partial(_ab_sample_kernel, li),
        out_shape=(
            jax.ShapeDtypeStruct((R_ALL, D_MODEL), f32),
            jax.ShapeDtypeStruct((DEC_BATCH, CONV_W - 1, C_CONV), f32),
            jax.ShapeDtypeStruct((DEC_BATCH, WINDOW, KV_W), f32),
            jax.ShapeDtypeStruct((DEC_BATCH, WINDOW, KV_W), f32),
        ),
        grid=(steps,),
        in_specs=[
            pl.BlockSpec(memory_space=pltpu.SMEM),
            pl.BlockSpec((SB_SAMPLE, D_MODEL), lambda i: (first_blk + i, 0)),
            _layer((3, D_MODEL), layer),
            _layer((D_MODEL, IN_AB), li),
            _layer((ATT_W + C_CONV, D_MODEL), li),
            _layer((CONV_W, C_CONV), li),
            _resident((N_EVEN, C_CONV)),
            _resident((N_EVEN, C_CONV)),
            _resident((N_EVEN, C_CONV)),
            _resident((1, LANES)),
            _resident((1, LANES)),
            pl.BlockSpec((None, SB_SAMPLE, CONV_W - 1, C_CONV), lambda i: (li, i, 0, 0)),
            pl.BlockSpec((SB_SAMPLE, WINDOW, KV_W), seq3),
            pl.BlockSpec((SB_SAMPLE, WINDOW, KV_W), seq3),
        ],
        out_specs=(
            pl.BlockSpec((SB_SAMPLE, D_MODEL), lambda i: (first_blk + i, 0)),
            pl.BlockSpec((SB_SAMPLE, CONV_W - 1, C_CONV), seq3),
            pl.BlockSpec((SB_SAMPLE, WINDOW, KV_W), seq3),
            pl.BlockSpec((SB_SAMPLE, WINDOW, KV_W), seq3),
        ),
        scratch_shapes=[
            pltpu.VMEM((SB_SAMPLE, N_HEADS * LANES), f32),
            pltpu.VMEM((SB_SAMPLE, KV_W), f32),
            pltpu.VMEM((SB_SAMPLE, KV_W), f32),
            pltpu.VMEM((SB_SAMPLE, C_CONV), f32),
            pltpu.VMEM((C_CONV // LANES * SB_SAMPLE * SUBLANES, LANES), f32),
            pltpu.VMEM((SB_SAMPLE * N_HEADS, LANES), f32),
        ],
        input_output_aliases={1: 0},
        compiler_params=_cparams(1),
        name="ab_sample",
    )(sink, x, norm_g, w_in, w_out, cw, cb, lng, lnb, cos_s, sin_s, ctx, kc, vc)


def _ssm_prep_kernel(are_ref, aim_ref, ldt_ref, bre_ref, bim_ref, lre_ref, lim_ref, pre_ref, pim_ref, bbre_ref, bbim_ref):
    a_re = are_ref[...]
    a_im = aim_ref[...]
    dt = jnp.exp(ldt_ref[...])
    mag = jnp.exp(a_re * dt)
    ang = a_im * dt
    lam_re = mag * jnp.cos(ang)
    lam_im = mag * jnp.sin(ang)
    den = a_re * a_re + a_im * a_im
    nr = lam_re - 1.0
    f_re = (nr * a_re + lam_im * a_im) / den
    f_im = (lam_im * a_re - nr * a_im) / den
    b_re = bre_ref[...]
    b_im = bim_ref[...]
    bbre_ref[...] = f_re * b_re - f_im * b_im
    bbim_ref[...] = f_re * b_im + f_im * b_re
    lre_ref[...] = lam_re
    lim_ref[...] = lam_im
    p_re, p_im = lam_re, lam_im
    r_re = r_im = None
    n = SEG_LEN
    while n:
        if n & 1:
            if r_re is None:
                r_re, r_im = p_re, p_im
            else:
                r_re, r_im = r_re * p_re - r_im * p_im, r_re * p_im + r_im * p_re
        n >>= 1
        if n:
            p_re, p_im = p_re * p_re - p_im * p_im, 2.0 * p_re * p_im
    pre_ref[...] = r_re
    pim_ref[...] = r_im


def _ssm_prep(a_re, a_im, log_dt, b_re, b_im, c_re, c_im):
    flat = lambda t: t.reshape(1, N_CPLX)
    ldt = jnp.repeat(log_dt, SSM_STATE).reshape(1, N_CPLX)
    b_t = lambda t: t.transpose(2, 0, 1).reshape(SSM_GROUP, N_CPLX)
    row = jax.ShapeDtypeStruct((1, N_CPLX), f32)
    mat = jax.ShapeDtypeStruct((SSM_GROUP, N_CPLX), f32)
    lre, lim, pre, pim, bbre, bbim = pl.pallas_call(
        _ssm_prep_kernel, out_shape=(row, row, row, row, mat, mat), name="ssm_prep",
    )(flat(a_re), flat(a_im), ldt, b_t(b_re), b_t(b_im))
    eye = jnp.eye(SSM_GROUP, dtype=f32)
    gq = N_GROUPS // N_QUART

    def in_mat(bb):
        t = bb.reshape(SSM_GROUP, N_QUART, gq, SSM_STATE).transpose(1, 2, 0, 3)
        return (t[:, :, :, None, :] * eye[None, :, None, :, None]).reshape(N_QUART, Q_CH, Q_ST).astype(bf16)

    def out_mat(cc):
        t = cc.reshape(N_QUART, gq, SSM_GROUP, SSM_STATE).transpose(0, 1, 3, 2)
        return (t[:, :, :, None, :] * eye[None, :, None, :, None]).reshape(N_QUART, Q_ST, Q_CH).astype(bf16)

    return lre, lim, pre, pim, in_mat(bbre), in_mat(bbim), out_mat(c_re), out_mat(c_im)


def _cmul_add(lr, li, hr, hi, xr, xi):
    return lr * hr - li * hi + xr, lr * hi + li * hr + xi


def _static_loop(n, body, carry):
    for t in range(n):
        carry = body(t, carry)
    return carry


def _ssm_prompt_kernel(lidx, x_ref, g_ref, win_ref, bbre_ref, bbim_ref, ccre_ref, ccim_ref, lre_ref, lim_ref, pre_ref,
                       pim_ref, d_ref, wglu_ref, o_ref, sre_ref, sim_ref, slab_ref, perm_ref, xr_ref, xi_ref, y_ref,
                       cr_ref, ci_ref):
    tt = TT_SSM
    i = pl.program_id(1)

    @pl.when(i == 0)
    def _():
        cr_ref[...] = jnp.zeros_like(cr_ref)
        ci_ref[...] = jnp.zeros_like(ci_ref)

    x = x_ref[...]
    h = _rms(x, g_ref[1:2, :]).astype(bf16)
    u = _dot(h, win_ref[...].astype(bf16))

    n_slab = D_MODEL // LANES
    for ks in range(n_slab):
        slab_ref[ks * tt:(ks + 1) * tt, :] = u[:, ks * LANES:(ks + 1) * LANES]

    def gather(t, carry):
        for ks in range(n_slab):
            perm_ref[pl.ds(t * SCAN_SEGS, SCAN_SEGS), ks * LANES:(ks + 1) * LANES] = (
                slab_ref[pl.ds(ks * tt + t, SCAN_SEGS, stride=SEG_LEN), :])
        return carry

    _static_loop(SEG_LEN, gather, 0)
    up = perm_ref[...]
    ub = up.astype(bf16)

    for qd in range(N_QUART):
        st = slice(qd * Q_ST, (qd + 1) * Q_ST)
        ubq = ub[:, qd * Q_CH:(qd + 1) * Q_CH]
        xr_ref[...] = _dot(ubq, bbre_ref[qd])
        xi_ref[...] = _dot(ubq, bbim_ref[qd])
        lr = jnp.broadcast_to(lre_ref[:, st], (SCAN_SEGS, Q_ST))
        li = jnp.broadcast_to(lim_ref[:, st], (SCAN_SEGS, Q_ST))

        def local_end(t, hc):
            rows = pl.ds(t * SCAN_SEGS, SCAN_SEGS)
            return _cmul_add(lr, li, hc[0], hc[1], xr_ref[rows, :], xi_ref[rows, :])

        zero = jnp.zeros((SCAN_SEGS, Q_ST), f32)
        er, ei = _static_loop(SEG_LEN, local_end, (zero, zero))

        pr = pre_ref[:, st]
        pi = pim_ref[:, st]
        c_r = cr_ref[:, st]
        c_i = ci_ref[:, st]
        rows_r, rows_i = [c_r], [c_i]
        for j in range(1, SCAN_SEGS):
            c_r, c_i = _cmul_add(pr, pi, c_r, c_i, er[j - 1:j], ei[j - 1:j])
            rows_r.append(c_r)
            rows_i.append(c_i)
        c_r, c_i = _cmul_add(pr, pi, c_r, c_i, er[SCAN_SEGS - 1:], ei[SCAN_SEGS - 1:])
        cr_ref[:, st] = c_r
        ci_ref[:, st] = c_i
        start = (jnp.concatenate(rows_r, axis=0), jnp.concatenate(rows_i, axis=0))

        def scan(t, hc):
            rows = pl.ds(t * SCAN_SEGS, SCAN_SEGS)
            hr, hi = _cmul_add(lr, li, hc[0], hc[1], xr_ref[rows, :], xi_ref[rows, :])
            xr_ref[rows, :] = hr
            xi_ref[rows, :] = hi
            return hr, hi

        _static_loop(SEG_LEN, scan, start)
        y_ref[:, qd * Q_CH:(qd + 1) * Q_CH] = (_dot(xr_ref[...].astype(bf16), ccre_ref[qd])
                                               - _dot(xi_ref[...].astype(bf16), ccim_ref[qd]))

    y = y_ref[...] + d_ref[lidx:lidx + 1, :] * up
    vg = _dot(jax.nn.gelu(y).astype(bf16), wglu_ref[...].astype(bf16))
    perm_ref[...] = vg[:, :D_MODEL] * jax.nn.sigmoid(vg[:, D_MODEL:])

    def scatter(t, carry):
        for ks in range(n_slab):
            slab_ref[pl.ds(ks * tt + t, SCAN_SEGS, stride=SEG_LEN), :] = (
                perm_ref[pl.ds(t * SCAN_SEGS, SCAN_SEGS), ks * LANES:(ks + 1) * LANES])
        return carry

    _static_loop(SEG_LEN, scatter, 0)
    out = jnp.concatenate([slab_ref[ks * tt:(ks + 1) * tt, :] for ks in range(n_slab)], axis=1)
    row = i * tt + lax.broadcasted_iota(jnp.int32, (tt, 1), 0)
    o_ref[...] = jnp.where(row >= PAD_ROWS, x + out, 0.0)

    @pl.when(i == pl.num_programs(1) - 1)
    def _():
        sre_ref[0] = cr_ref[...]
        sim_ref[0] = ci_ref[...]


def _ssm_prompt(x, li, layer, norm_g, w_in, bbre, bbim, ccre, ccim, lre, lim, pre, pim, d, w_glu):
    steps = ROWS_B // TT_SSM
    state = jax.ShapeDtypeStruct((BATCH, 1, N_CPLX), f32)
    return pl.pallas_call(
        functools.partial(_ssm_prompt_kernel, li),
        out_shape=(jax.ShapeDtypeStruct((R_ALL, D_MODEL), f32), state, state),
        grid=(BATCH, steps),
        in_specs=[
            pl.BlockSpec((TT_SSM, D_MODEL), lambda b, i: (b * steps + i, 0)),
            _layer((3, D_MODEL), layer),
            _layer((D_MODEL, D_MODEL), li),
            _resident((N_QUART, Q_CH, Q_ST)),
            _resident((N_QUART, Q_CH, Q_ST)),
            _resident((N_QUART, Q_ST, Q_CH)),
            _resident((N_QUART, Q_ST, Q_CH)),
            _resident((1, N_CPLX)),
            _resident((1, N_CPLX)),
            _resident((1, N_CPLX)),
            _resident((1, N_CPLX)),
            _resident((N_ODD, D_MODEL)),
            _layer((D_MODEL, 2 * D_MODEL), li),
        ],
        out_specs=(
            pl.BlockSpec((TT_SSM, D_MODEL), lambda b, i: (b * steps + i, 0)),
            pl.BlockSpec((1, 1, N_CPLX), lambda b, i: (b, 0, 0)),
            pl.BlockSpec((1, 1, N_CPLX), lambda b, i: (b, 0, 0)),
        ),
        scratch_shapes=[
            pltpu.VMEM((D_MODEL // LANES * TT_SSM, LANES), f32),
            pltpu.VMEM((TT_SSM, D_MODEL), f32),
            pltpu.VMEM((TT_SSM, Q_ST), f32),
            pltpu.VMEM((TT_SSM, Q_ST), f32),
            pltpu.VMEM((TT_SSM, D_MODEL), f32),
            pltpu.VMEM((1, N_CPLX), f32),
            pltpu.VMEM((1, N_CPLX), f32),
        ],
        input_output_aliases={0: 0},
        compiler_params=_cparams(2),
        name="ssm_prompt",
    )(x, norm_g, w_in, bbre, bbim, ccre, ccim, lre, lim, pre, pim, d, w_glu)


def _ssm_sample_kernel(li, x_ref, g_ref, win_ref, bbre_ref, bbim_ref, ccre_ref, ccim_ref, lre_ref, lim_ref, d_ref,
                       wglu_ref, sre_in, sim_in, o_ref, sre_ref, sim_ref, y_ref):
    x = x_ref[...]
    h = _rms(x, g_ref[1:2, :]).astype(bf16)
    u = _dot(h, win_ref[...].astype(bf16))
    ub = u.astype(bf16)
    for qd in range(N_QUART):
        st = slice(qd * Q_ST, (qd + 1) * Q_ST)
        ubq = ub[:, qd * Q_CH:(qd + 1) * Q_CH]
        hr, hi = _cmul_add(lre_ref[:, st], lim_ref[:, st], sre_in[:, st], sim_in[:, st],
                           _dot(ubq, bbre_ref[qd]), _dot(ubq, bbim_ref[qd]))
        sre_ref[:, st] = hr
        sim_ref[:, st] = hi
        y_ref[:, qd * Q_CH:(qd + 1) * Q_CH] = _dot(hr.astype(bf16), ccre_ref[qd]) - _dot(hi.astype(bf16), ccim_ref[qd])
    y = y_ref[...] + d_ref[li:li + 1, :] * u
    vg = _dot(jax.nn.gelu(y).astype(bf16), wglu_ref[...].astype(bf16))
    o_ref[...] = x + vg[:, :D_MODEL] * jax.nn.sigmoid(vg[:, D_MODEL:])


def _ssm_sample(x, li, layer, norm_g, w_in, bbre, bbim, ccre, ccim, lre, lim, d, w_glu, s_re, s_im):
    state = jax.ShapeDtypeStruct((DEC_BATCH, N_CPLX), f32)
    blk = R_PROMPT // DEC_BATCH
    return pl.pallas_call(
        functools.partial(_ssm_sample_kernel, li),
        out_shape=(jax.ShapeDtypeStruct((R_ALL, D_MODEL), f32), state, state),
        grid=(1,),
        in_specs=[
            pl.BlockSpec((DEC_BATCH, D_MODEL), lambda i: (blk, 0)),
            _layer((3, D_MODEL), layer),
            _layer((D_MODEL, D_MODEL), li),
            _resident((N_QUART, Q_CH, Q_ST)),
            _resident((N_QUART, Q_CH, Q_ST)),
            _resident((N_QUART, Q_ST, Q_CH)),
            _resident((N_QUART, Q_ST, Q_CH)),
            _resident((1, N_CPLX)),
            _resident((1, N_CPLX)),
            _resident((N_ODD, D_MODEL)),
            _layer((D_MODEL, 2 * D_MODEL), li),
            _layer((DEC_BATCH, N_CPLX), li),
            _layer((DEC_BATCH, N_CPLX), li),
        ],
        out_specs=(
            pl.BlockSpec((DEC_BATCH, D_MODEL), lambda i: (blk, 0)),
            pl.BlockSpec((DEC_BATCH, N_CPLX), lambda i: (0, 0)),
            pl.BlockSpec((DEC_BATCH, N_CPLX), lambda i: (0, 0)),
        ),
        scratch_shapes=[pltpu.VMEM((DEC_BATCH, D_MODEL), f32)],
        input_output_aliases={0: 0},
        compiler_params=_cparams(1),
        name="ssm_sample",
    )(x, norm_g, w_in, bbre, bbim, ccre, ccim, lre, lim, d, w_glu, s_re, s_im)


def _rope_tables():
    half = ROPE_DIM // 2
    inv = ROPE_THETA ** (-jnp.arange(half, dtype=f32) * 2.0 / ROPE_DIM)
    pos = jnp.concatenate([jnp.arange(ROWS_B, dtype=jnp.int32) - PAD_ROWS, jnp.full((1,), PAST_LEN, jnp.int32)])
    ang = pos.astype(f32)[:, None] * inv[None, :]
    cos, sin = jnp.cos(ang), jnp.sin(ang)
    ones = jnp.ones((pos.shape[0], HEAD_DIM - ROPE_DIM), f32)
    cos_h = jnp.concatenate([cos, cos, ones], axis=1)
    sin_h = jnp.concatenate([-sin, sin, 0.0 * ones], axis=1)
    cos_t = jnp.concatenate([cos_h, cos_h], axis=1)
    sin_t = jnp.concatenate([sin_h, sin_h], axis=1)
    return cos_t[:ROWS_B], sin_t[:ROWS_B], cos_t[ROWS_B:], sin_t[ROWS_B:]


def _window_bias():
    r = jnp.arange(BLOCK, dtype=jnp.int32)[:, None]
    s = jnp.arange(2 * BLOCK, dtype=jnp.int32)[None, :]
    return jnp.where((s >= r) & (s <= r + WINDOW), 0.0, NEG_BIG).astype(f32)


def kernel(x_prompt, x_sample, state_conv, cache_win_k, cache_win_v, state_ssm_re, state_ssm_im, meta_tokens, norm_g, final_norm_g, ffn1_w_gu, ffn1_w_down, ffn2_w_gu, ffn2_w_down, ab_w_in, ab_w_out, attn_sink, conv_w, conv_b, conv_ln_g, conv_ln_b, ssm_w_in, ssm_a_re, ssm_a_im, ssm_log_dt, ssm_b_re, ssm_b_im, ssm_c_re, ssm_c_im, ssm_d, ssm_w_glu):
    meta = jnp.broadcast_to(meta_tokens[None], (BATCH, N_META, D_MODEL))
    pad = jnp.zeros((BATCH, PAD_ROWS, D_MODEL), f32)
    x = jnp.concatenate([jnp.concatenate([pad, meta, x_prompt], axis=1).reshape(R_PROMPT, D_MODEL),
                         x_sample.reshape(DEC_BATCH, D_MODEL)], axis=0)

    cos_t, sin_t, cos_s, sin_s = _rope_tables()
    bias = _window_bias()
    row = lambda t: t.reshape(1, -1)

    p_conv, p_k, p_v, p_re, p_im = [], [], [], [], []
    s_conv, s_k, s_v, s_re, s_im = [], [], [], [], []
    s_re_in = state_ssm_re.reshape(N_ODD, DEC_BATCH, N_CPLX)
    s_im_in = state_ssm_im.reshape(N_ODD, DEC_BATCH, N_CPLX)
    for l in range(DEPTH):
        x = _ffn(x, norm_g, ffn1_w_gu, ffn1_w_down, l, 0)
        i = l // 2
        if l % 2 == 0:
            wargs = (attn_sink, norm_g, ab_w_in, ab_w_out, conv_w, conv_b, conv_ln_g, conv_ln_b)
            x, cst, kst, vst = _ab_prompt(x, i, l, *wargs, cos_t, sin_t, bias)
            x, cso, kso, vso = _ab_sample(x, i, l, *wargs, cos_s, sin_s, state_conv,
                                          cache_win_k[i].reshape(DEC_BATCH, WINDOW, KV_W),
                                          cache_win_v[i].reshape(DEC_BATCH, WINDOW, KV_W))
            p_conv.append(cst)
            p_k.append(kst.reshape(BATCH, WINDOW, N_KV, HEAD_DIM))
            p_v.append(vst.reshape(BATCH, WINDOW, N_KV, HEAD_DIM))
            s_conv.append(cso)
            s_k.append(kso.reshape(DEC_BATCH, WINDOW, N_KV, HEAD_DIM))
            s_v.append(vso.reshape(DEC_BATCH, WINDOW, N_KV, HEAD_DIM))
        else:
            lre, lim, pre, pim, bbre, bbim, ccre, ccim = _ssm_prep(
                ssm_a_re[i], ssm_a_im[i], ssm_log_dt[i], ssm_b_re[i], ssm_b_im[i], ssm_c_re[i], ssm_c_im[i])
            x, pre_s, pim_s = _ssm_prompt(x, i, l, norm_g, ssm_w_in, bbre, bbim, ccre, ccim, lre, lim, pre, pim,
                                          ssm_d, ssm_w_glu)
            x, sre_s, sim_s = _ssm_sample(x, i, l, norm_g, ssm_w_in, bbre, bbim, ccre, ccim, lre, lim,
                                          ssm_d, ssm_w_glu, s_re_in, s_im_in)
            p_re.append(pre_s.reshape(BATCH, N_GROUPS, SSM_STATE))
            p_im.append(pim_s.reshape(BATCH, N_GROUPS, SSM_STATE))
            s_re.append(sre_s.reshape(DEC_BATCH, N_GROUPS, SSM_STATE))
            s_im.append(sim_s.reshape(DEC_BATCH, N_GROUPS, SSM_STATE))
        x = _ffn(x, norm_g, ffn2_w_gu, ffn2_w_down, l, 2)

    y_prompt, y_sample = _final_norm(x, row(final_norm_g))
    return (y_prompt, y_sample,
            jnp.stack(p_conv), jnp.stack(p_k), jnp.stack(p_v), jnp.stack(p_re), jnp.stack(p_im),
            jnp.stack(s_conv), jnp.stack(s_k), jnp.stack(s_v), jnp.stack(s_re), jnp.stack(s_im))
```
